```python
import math
import jax
import jax.numpy as jnp
from jax import lax
import numpy as np

D_MODEL = 1024
BATCH = 8
SEQ = 2048
DEPTH = 4
DEC_BATCH = 128
DEC_SEQ = 1
PAST_LEN = 2048
PAGE_SIZE = 128

N_MIXERS = 4
HEADS_PER_MIXER = 4
N_HEADS = N_MIXERS * HEADS_PER_MIXER
HEAD_DIM = D_MODEL // N_HEADS
MIX_WIDTH = N_HEADS * HEAD_DIM
N_IN = 3 * MIX_WIDTH + HEADS_PER_MIXER
D_FF = ((8 * D_MODEL // 3 + 255) // 256) * 256
N_BUCKETS = 32
MAX_DISTANCE = 128
QBLOCK = 128
MOBA_BLOCK = 256
MOBA_TOPK = 3
MOBA_QBLOCK = 32
LN_EPS = 1e-5
RMS_EPS = 1e-5
DEEPNORM_ALPHA = (2 * DEPTH) ** 0.25
DEEPNORM_BETA = (8 * DEPTH) ** -0.25
NEG = -1e30

kernel_name = 'hybrid_sb_diff_fox_moba_decoder_step'


def _layernorm(x, g, b):
    xf = x.astype(jnp.float32)
    mu = jnp.mean(xf, axis=-1, keepdims=True)
    var = jnp.mean(jnp.square(xf - mu), axis=-1, keepdims=True)
    y = (xf - mu) * lax.rsqrt(var + LN_EPS) * g.astype(jnp.float32) + b.astype(jnp.float32)
    return y.astype(x.dtype)


def _swiglu(x, w_gu, w_down):
    gate, up = jnp.split(x @ w_gu, 2, axis=-1)
    return (jax.nn.silu(gate) * up) @ w_down


def _rel_bucket(dist):
    max_exact = N_BUCKETS // 2
    d = jnp.maximum(dist, 0)
    far = max_exact + (jnp.log(jnp.maximum(d, 1).astype(jnp.float32) / max_exact)
                       / math.log(MAX_DISTANCE / max_exact) * (N_BUCKETS - max_exact)).astype(jnp.int32)
    return jnp.where(d < max_exact, d, jnp.minimum(far, N_BUCKETS - 1))


def _sweep(fn, blk, xs):
    T = xs[0].shape[1]
    blk = min(blk, T)
    nb = -(-T // blk)
    pad = nb * blk - T

    def split(a):
        a = jnp.pad(a, [(0, 0), (0, pad)] + [(0, 0)] * (a.ndim - 2))
        return jnp.moveaxis(a.reshape(a.shape[0], nb, blk, *a.shape[2:]), 1, 0)

    out = lax.map(lambda args: fn(*args), tuple(split(a) for a in xs))
    out = jnp.moveaxis(out, 0, 1)
    return out.reshape(out.shape[0], nb * blk, *out.shape[3:])[:, :T]


def _stick_breaking(q, k, v, qpos, kpos):
    scale = HEAD_DIM ** -0.5
    kf, vf = k.astype(jnp.float32), v.astype(jnp.float32)

    def block(qb, pb):
        z = jnp.einsum('bqhd,bkhd->bhqk', qb.astype(jnp.float32), kf) * scale
        before = kpos[None, None, None, :] < pb[:, None, :, None]
        log_1m = jnp.where(before, jax.nn.log_sigmoid(-z), 0.0)
        rest = lax.cumsum(log_1m, axis=3, reverse=True) - log_1m
        w = jnp.where(before, jnp.exp(jax.nn.log_sigmoid(z) + rest), 0.0)
        return jnp.einsum('bhqk,bkhd->bqhd', w, vf)

    return _sweep(block, QBLOCK, (q, qpos))


def _differential(q, k, v, qpos, kpos, rel_bias_b, lam, lam_init, subln_g):
    dq = HEAD_DIM // 2
    scale = dq ** -0.5
    kf, vf = k.astype(jnp.float32), v.astype(jnp.float32)
    k1, k2 = kf[..., :dq], kf[..., dq:]
    g = subln_g.astype(jnp.float32)

    def block(qb, pb):
        qf = qb.astype(jnp.float32)
        dist = pb[0][:, None] - kpos[None, :]
        bias = jnp.transpose(rel_bias_b[_rel_bucket(dist)], (2, 0, 1))[None].astype(jnp.float32)
        causal = (dist >= 0)[None, None]

        def attn_map(qh, kh):
            s = jnp.einsum('bqhd,bkhd->bhqk', qh, kh) * scale + bias
            return jax.nn.softmax(jnp.where(causal, s, NEG), axis=-1)

        a = attn_map(qf[..., :dq], k1) - lam * attn_map(qf[..., dq:], k2)
        o = jnp.einsum('bhqk,bkhd->bqhd', a, vf)
        o = o * lax.rsqrt(jnp.mean(o * o, axis=-1, keepdims=True) + RMS_EPS)
        return o * g * (1.0 - lam_init)

    return _sweep(block, QBLOCK, (q, qpos))


def _forgetting(q, k, v, qpos, kpos, q_cumf, k_cumf):
    scale = HEAD_DIM ** -0.5
    kf, vf = k.astype(jnp.float32), v.astype(jnp.float32)
    kF = jnp.transpose(k_cumf, (0, 2, 1))[:, :, None, :]

    def block(qb, pb, fb):
        s = (jnp.einsum('bqhd,bkhd->bhqk', qb.astype(jnp.float32), kf) * scale
             + jnp.transpose(fb, (0, 2, 1))[..., None] - kF)
        causal = kpos[None, None, None, :] <= pb[:, None, :, None]
        p = jax.nn.softmax(jnp.where(causal, s, NEG), axis=-1)
        return jnp.einsum('bhqk,bkhd->bqhd', p, vf)

    return _sweep(block, QBLOCK, (q, qpos, q_cumf))


def _moba(q, k, v, qpos, rel_bias_d):
    B, S, H, Dh = k.shape
    nblk = -(-S // MOBA_BLOCK)
    pad = nblk * MOBA_BLOCK - S
    scale = Dh ** -0.5

    def blocks(a):
        a = jnp.pad(a.astype(jnp.float32), ((0, 0), (0, pad), (0, 0), (0, 0)))
        return jnp.transpose(a.reshape(B, nblk, MOBA_BLOCK, H, Dh), (0, 3, 1, 2, 4))

    kb, vb = blocks(k), blocks(v)
    kmean = jnp.mean(kb, axis=3)
    offs = jnp.arange(MOBA_BLOCK, dtype=jnp.int32)
    bidx = jnp.arange(B)[:, None, None, None]
    hidx = jnp.arange(H)[None, None, :, None]
    n_sel = min(MOBA_TOPK, nblk)

    def block(qb, pb):
        qf = qb.astype(jnp.float32)
        t = pb[0]
        QB = t.shape[0]
        cur = t // MOBA_BLOCK
        gate = jnp.einsum('bqhd,bhnd->bqhn', qf, kmean)
        fully_past = jnp.arange(nblk)[None, :] < cur[:, None]
        gate = jnp.where(fully_past[None, :, None, :], gate, NEG)
        _, sel = lax.top_k(gate, n_sel)
        sel_ok = sel < cur[None, :, None, None]
        ks = kb[bidx, hidx, sel]
        vs = vb[bidx, hidx, sel]
        sel_pos = sel[..., None] * MOBA_BLOCK + offs
        bias_sel = rel_bias_d[_rel_bucket(t[None, :, None, None, None] - sel_pos), hidx[..., None]]
        s_sel = jnp.einsum('bqhd,bqhknd->bqhkn', qf, ks) * scale + bias_sel.astype(jnp.float32)
        s_sel = jnp.where(sel_ok[..., None], s_sel, NEG)
        c0 = cur[0]
        k_own = lax.dynamic_index_in_dim(kb, c0, axis=2, keepdims=False)
        v_own = lax.dynamic_index_in_dim(vb, c0, axis=2, keepdims=False)
        d_own = t[:, None] - (c0 * MOBA_BLOCK + offs)[None, :]
        bias_own = jnp.transpose(rel_bias_d[_rel_bucket(d_own)], (0, 2, 1))[None].astype(jnp.float32)
        s_own = jnp.einsum('bqhd,bhnd->bqhn', qf, k_own) * scale + bias_own
        s_own = jnp.where((d_own >= 0)[None, :, None, :], s_own, NEG)
        n_g = n_sel * MOBA_BLOCK
        p = jax.nn.softmax(jnp.concatenate([s_sel.reshape(B, QB, H, n_g), s_own], axis=-1), axis=-1)
        p_sel = p[..., :n_g].reshape(B, QB, H, n_sel, MOBA_BLOCK)
        return (jnp.einsum('bqhkn,bqhknd->bqhd', p_sel, vs)
                + jnp.einsum('bqhn,bhnd->bqhd', p[..., n_g:], v_own))

    return _sweep(block, MOBA_QBLOCK, (q, qpos))


def _token_mix(h, lyr, past, w_in, b_f, w_out, rel_bias, diff_lambda, diff_subln_g):
    B, T, _ = h.shape
    proj = h @ w_in[lyr]
    q, k, v = [proj[..., i * MIX_WIDTH:(i + 1) * MIX_WIDTH].reshape(B, T, N_HEADS, HEAD_DIM)
               for i in range(3)]
    logf = jax.nn.log_sigmoid(proj[..., 3 * MIX_WIDTH:].astype(jnp.float32)
                              + b_f[lyr].astype(jnp.float32))
    n_past = 0 if past is None else past[3].shape[1] * PAGE_SIZE
    kpos = jnp.arange(n_past + T, dtype=jnp.int32)
    qpos = (n_past + jnp.arange(T, dtype=jnp.int32))[None]

    def rows(new, pool_idx, h0, h1):
        cur = new[:, :, h0:h1]
        if past is None:
            return cur
        old = past[pool_idx][lyr, past[3], :, h0:h1]
        old = old.reshape(B, n_past, *old.shape[3:])
        return jnp.concatenate([old.astype(cur.dtype), cur], axis=1)

    hs = [(g * HEADS_PER_MIXER, (g + 1) * HEADS_PER_MIXER) for g in range(N_MIXERS)]
    (a0, a1), (b0, b1), (c0, c1), (d0, d1) = hs

    o_a = _stick_breaking(q[:, :, a0:a1], rows(k, 0, a0, a1), rows(v, 1, a0, a1), qpos, kpos)

    lam_init = 0.8 - 0.6 * math.exp(-0.3 * lyr)
    lq1, lk1, lq2, lk2 = [diff_lambda[lyr, i].astype(jnp.float32) for i in range(4)]
    lam = jnp.exp(jnp.sum(lq1 * lk1)) - jnp.exp(jnp.sum(lq2 * lk2)) + lam_init
    o_b = _differential(q[:, :, b0:b1], rows(k, 0, b0, b1), rows(v, 1, b0, b1), qpos, kpos,
                        rel_bias[:, :HEADS_PER_MIXER], lam, lam_init, diff_subln_g[lyr])

    cum_f = jnp.cumsum(rows(logf, 2, 0, HEADS_PER_MIXER), axis=1)
    o_c = _forgetting(q[:, :, c0:c1], rows(k, 0, c0, c1), rows(v, 1, c0, c1), qpos, kpos,
                      cum_f[:, n_past:], cum_f)

    o_d = _moba(q[:, :, d0:d1], rows(k, 0, d0, d1), rows(v, 1, d0, d1), qpos,
                rel_bias[:, HEADS_PER_MIXER:])

    o = jnp.concatenate([o_a, o_b, o_c, o_d], axis=2).reshape(B, T, MIX_WIDTH).astype(h.dtype)
    return o @ w_out[lyr], k, v, logf


def _trunk(x, past, w_in, b_f, w_out, ffn_w_gu, ffn_w_down, ln_g, ln_b, rel_bias, diff_lambda, diff_subln_g):
    ks, vs, lfs = [], [], []
    for lyr in range(DEPTH):
        x = _layernorm(DEEPNORM_ALPHA * x + 0.5 * _swiglu(x, ffn_w_gu[lyr, 0], ffn_w_down[lyr, 0]),
                       ln_g[lyr, 0], ln_b[lyr, 0])
        mix, k, v, lf = _token_mix(x, lyr, past, w_in, b_f, w_out, rel_bias, diff_lambda, diff_subln_g)
        x = _layernorm(DEEPNORM_ALPHA * x + mix, ln_g[lyr, 1], ln_b[lyr, 1])
        x = _layernorm(DEEPNORM_ALPHA * x + 0.5 * _swiglu(x, ffn_w_gu[lyr, 1], ffn_w_down[lyr, 1]),
                       ln_g[lyr, 2], ln_b[lyr, 2])
        ks.append(k)
        vs.append(v)
        lfs.append(lf)
    return x, jnp.stack(ks), jnp.stack(vs), jnp.stack(lfs)


def setup_inputs(seed: int = 0) -> dict:
    key = jax.random.key(seed)
    kk = jax.random.split(key, 16)
    n_pages = PAST_LEN // PAGE_SIZE
    n_used = DEC_BATCH * n_pages
    n_pool = n_used + n_used // 4
    f32 = jnp.float32
    nrm = jax.random.normal
    x_prompt = nrm(kk[0], (BATCH, SEQ, D_MODEL), f32)
    x_sample = nrm(kk[1], (DEC_BATCH, DEC_SEQ, D_MODEL), f32)
    cache_k = nrm(kk[2], (DEPTH, n_pool, PAGE_SIZE, N_HEADS, HEAD_DIM), f32)
    cache_v = nrm(kk[3], (DEPTH, n_pool, PAGE_SIZE, N_HEADS, HEAD_DIM), f32)
    cache_logf = jax.nn.log_sigmoid(nrm(kk[4], (DEPTH, n_pool, PAGE_SIZE, HEADS_PER_MIXER), f32) + 3.0)
    page_table = jax.random.permutation(kk[5], n_pool)[:n_used].reshape(DEC_BATCH, n_pages).astype(jnp.int32)
    w_in = nrm(kk[6], (DEPTH, D_MODEL, N_IN), f32) * D_MODEL ** -0.5
    b_f = jax.random.uniform(kk[7], (DEPTH, HEADS_PER_MIXER), f32, 1.0, 4.0)
    w_out = nrm(kk[8], (DEPTH, MIX_WIDTH, D_MODEL), f32) * (MIX_WIDTH ** -0.5 * DEEPNORM_BETA)
    ffn_w_gu = nrm(kk[9], (DEPTH, 2, D_MODEL, 2 * D_FF), f32) * D_MODEL ** -0.5
    ffn_w_down = nrm(kk[10], (DEPTH, 2, D_FF, D_MODEL), f32) * (D_FF ** -0.5 * DEEPNORM_BETA)
    ln_g = 1.0 + 0.02 * nrm(kk[11], (DEPTH, 3, D_MODEL), f32)
    ln_b = 0.02 * nrm(kk[12], (DEPTH, 3, D_MODEL), f32)
    rel_bias = 0.5 * nrm(kk[13], (N_BUCKETS, 2 * HEADS_PER_MIXER), f32)
    diff_lambda = 0.1 * nrm(kk[14], (DEPTH, 4, HEAD_DIM // 2), f32)
    diff_subln_g = 1.0 + 0.02 * nrm(kk[15], (DEPTH, HEAD_DIM), f32)
    return {'x_prompt': x_prompt, 'x_sample': x_sample, 'cache_k': cache_k, 'cache_v': cache_v,
            'cache_logf': cache_logf, 'page_table': page_table, 'w_in': w_in, 'b_f': b_f,
            'w_out': w_out, 'ffn_w_gu': ffn_w_gu, 'ffn_w_down': ffn_w_down, 'ln_g': ln_g,
            'ln_b': ln_b, 'rel_bias': rel_bias, 'diff_lambda': diff_lambda,
            'diff_subln_g': diff_subln_g}


def reference(x_prompt, x_sample, cache_k, cache_v, cache_logf, page_table, w_in, b_f, w_out,
              ffn_w_gu, ffn_w_down, ln_g, ln_b, rel_bias, diff_lambda, diff_subln_g):
    y_prompt, k_prompt, v_prompt, logf_prompt = _trunk(
        x_prompt, None, w_in, b_f, w_out, ffn_w_gu, ffn_w_down, ln_g, ln_b, rel_bias,
        diff_lambda, diff_subln_g)
    y_sample, k_sample, v_sample, logf_sample = _trunk(
        x_sample, (cache_k, cache_v, cache_logf, page_table), w_in, b_f, w_out, ffn_w_gu,
        ffn_w_down, ln_g, ln_b, rel_bias, diff_lambda, diff_subln_g)
    return (y_prompt, y_sample, k_prompt, v_prompt, logf_prompt, k_sample, v_sample, logf_sample)
```

```python
import functools
import math

import jax
import jax.numpy as jnp
from jax import lax
from jax.experimental import pallas as pl
from jax.experimental.pallas import tpu as pltpu

F32 = jnp.float32
BF16 = jnp.bfloat16

N_MIXERS = 4
HEADS_PER_MIXER = 4
N_HEADS = N_MIXERS * HEADS_PER_MIXER
HEAD_DIM = 64
MIX_COLS = HEADS_PER_MIXER * HEAD_DIM
N_BUCKETS = 32
MAX_DISTANCE = 128
MOBA_BLOCK = 256
MOBA_TOPK = 3
LN_EPS = 1e-5
RMS_EPS = 1e-5
NEG = -1e30

LANES = 128
GATE_COLS = LANES
VMEM_LIMIT = 56 * 2 ** 20
ATT_BLOCK = MOBA_BLOCK
DEC_ROWS = 32

NT_DIMS = (((1,), (1,)), ((), ()))


def _cparams(n_axes):
    return pltpu.CompilerParams(dimension_semantics=("arbitrary",) * n_axes,
                                vmem_limit_bytes=VMEM_LIMIT)


def _resident(block_shape, index_map):
    return pl.BlockSpec(block_shape, index_map, pipeline_mode=pl.Buffered(1))


def _layernorm(y, g, b):
    mu = jnp.mean(y, axis=-1, keepdims=True)
    d = y - mu
    var = jnp.mean(d * d, axis=-1, keepdims=True)
    return d * lax.rsqrt(var + LN_EPS) * g + b


def _log_sigmoid_pair(z):
    lg = jnp.log1p(jnp.exp(-jnp.abs(z)))
    ls = jnp.minimum(z, 0.0) - lg
    return ls, ls - z


def _split_bf16(x, n):
    parts = []
    r = x
    for i in range(n):
        hi = r.astype(BF16)
        parts.append(hi)
        if i + 1 < n:
            r = r - hi.astype(F32)
    return parts


def _dot_exact_rhs(x, u, n):
    out = None
    for part in _split_bf16(x, n):
        d = jnp.dot(part, u, preferred_element_type=F32)
        out = d if out is None else out + d
    return out


def _dot_f32(a, b, dims):
    a1, a2, a3 = _split_bf16(a, 3)
    b1, b2, b3 = _split_bf16(b, 3)

    def d(x, y):
        return lax.dot_general(x, y, dims, preferred_element_type=F32)

    return d(a1, b1) + (d(a1, b2) + d(a2, b1)) + (d(a1, b3) + d(a2, b2) + d(a3, b1))


def _topk_select(gates, n_valid, n_max, lane):
    valid = lane < n_valid
    gv = jnp.where(valid, gates, NEG)
    rank = jnp.zeros(gates.shape, F32)
    for m in range(n_max):
        col = gv[:, m:m + 1]
        beats = (col > gv) | ((col == gv) & (m < lane))
        rank = rank + beats.astype(F32)
    return jnp.where((rank < MOBA_TOPK) & valid, 1.0, 0.0)


def _ffn_kernel(x_ref, wgu_ref, wd_ref, g_ref, b_ref, o_ref, *, d_ff, chunk, alpha):
    x = x_ref[...]
    xb = x.astype(BF16)
    acc = jnp.zeros(x.shape, F32)
    for c in range(d_ff // chunk):
        gate = jnp.dot(xb, wgu_ref[:, c * chunk:(c + 1) * chunk], preferred_element_type=F32)
        up = jnp.dot(xb, wgu_ref[:, d_ff + c * chunk:d_ff + (c + 1) * chunk],
                     preferred_element_type=F32)
        h = (gate * jax.nn.sigmoid(gate) * up).astype(BF16)
        acc = acc + jnp.dot(h, wd_ref[c * chunk:(c + 1) * chunk, :], preferred_element_type=F32)
    o_ref[...] = _layernorm(alpha * x + 0.5 * acc, g_ref[...], b_ref[...])


def _ffn_ln(x, wgu, wd, ln_g, ln_b, lyr, idx, alpha, tm):
    n, d = x.shape
    d_ff = wd.shape[2]
    chunk = 256
    assert n % tm == 0 and d_ff % chunk == 0
    return pl.pallas_call(
        functools.partial(_ffn_kernel, d_ff=d_ff, chunk=chunk, alpha=alpha),
        grid=(n // tm,),
        in_specs=[
            pl.BlockSpec((tm, d), lambda i: (i, 0)),
            _resident((None, None, d, 2 * d_ff), lambda i: (lyr, idx, 0, 0)),
            _resident((None, None, d_ff, d), lambda i: (lyr, idx, 0, 0)),
            _resident((None, 1, d), lambda i: (lyr * 3 + 2 * idx, 0, 0)),
            _resident((None, 1, d), lambda i: (lyr * 3 + 2 * idx, 0, 0)),
        ],
        out_specs=pl.BlockSpec((tm, d), lambda i: (i, 0)),
        out_shape=jax.ShapeDtypeStruct((n, d), F32),
        compiler_params=_cparams(1),
        name="ffn_ln",
    )(x, wgu, wd, ln_g, ln_b)


def _inproj_kernel(x_ref, w_ref, wf_ref, bf_ref, qb_ref, q32_ref, kf_ref, vf_ref, kb_ref, vb_ref,
                   lf_ref, cum_ref, cumt_ref, carry_ref, *, width, tiles_per_seq, q_cols):
    i = pl.program_id(0)
    xb = x_ref[...].astype(BF16)
    q = jnp.dot(xb, w_ref[:, 0:width], preferred_element_type=F32)
    qb_ref[...] = q.astype(BF16)
    q32_ref[...] = q[:, width - q_cols:]
    k = jnp.dot(xb, w_ref[:, width:2 * width], preferred_element_type=F32)
    kf_ref[...] = k
    kb_ref[...] = k.astype(BF16)
    v = jnp.dot(xb, w_ref[:, 2 * width:3 * width], preferred_element_type=F32)
    vf_ref[...] = v
    vb_ref[...] = v.astype(BF16)
    z = jnp.dot(xb, wf_ref[...], preferred_element_type=F32) + bf_ref[...]
    lf, _ = _log_sigmoid_pair(z)
    lf_ref[...] = lf

    @pl.when(i % tiles_per_seq == 0)
    def _():
        carry_ref[...] = jnp.zeros_like(carry_ref)

    tm = lf.shape[0]
    row = lax.broadcasted_iota(jnp.int32, (tm, tm), 0)
    col = lax.broadcasted_iota(jnp.int32, (tm, tm), 1)
    lower = (col <= row).astype(BF16)
    cum = carry_ref[0:1, :]
    for part in _split_bf16(lf, 3):
        cum = cum + jnp.dot(lower, part, preferred_element_type=F32)
    carry_ref[...] = jnp.broadcast_to(cum[tm - 1:tm, :], carry_ref.shape)
    cum_ref[...] = cum
    cumt_ref[0] = cum.T[0:8, :]


def _inproj(x, w_qkv, w_f, b_f, lyr, tm, seq_len):
    n, d = x.shape
    width = w_qkv.shape[2] // 3
    assert n % tm == 0 and seq_len % tm == 0
    tiles_per_seq = seq_len // tm
    n_seq = n // seq_len
    row_spec = lambda cols: pl.BlockSpec((tm, cols), lambda i: (i, 0))
    outs = pl.pallas_call(
        functools.partial(_inproj_kernel, width=width, tiles_per_seq=tiles_per_seq, q_cols=MIX_COLS),
        grid=(n // tm,),
        in_specs=[
            pl.BlockSpec((tm, d), lambda i: (i, 0)),
            _resident((None, d, 3 * width), lambda i: (lyr, 0, 0)),
            _resident((None, d, GATE_COLS), lambda i: (lyr, 0, 0)),
            _resident((None, 1, GATE_COLS), lambda i: (lyr, 0, 0)),
        ],
        out_specs=[
            row_spec(width), row_spec(MIX_COLS), row_spec(width), row_spec(width),
            row_spec(width), row_spec(width), row_spec(GATE_COLS), row_spec(GATE_COLS),
            pl.BlockSpec((1, 8, tm), lambda i: (i // tiles_per_seq, 0, i % tiles_per_seq)),
        ],
        out_shape=[
            jax.ShapeDtypeStruct((n, width), BF16),
            jax.ShapeDtypeStruct((n, MIX_COLS), F32),
            jax.ShapeDtypeStruct((n, width), F32),
            jax.ShapeDtypeStruct((n, width), F32),
            jax.ShapeDtypeStruct((n, width), BF16),
            jax.ShapeDtypeStruct((n, width), BF16),
            jax.ShapeDtypeStruct((n, GATE_COLS), F32),
            jax.ShapeDtypeStruct((n, GATE_COLS), F32),
            jax.ShapeDtypeStruct((n_seq, 8, seq_len), F32),
        ],
        scratch_shapes=[pltpu.VMEM((8, GATE_COLS), F32)],
        compiler_params=_cparams(1),
        name="inproj",
    )(x, w_qkv, w_f, b_f)
    return outs


def _inproj_dec_kernel(x_ref, w_ref, wf_ref, bf_ref, q_ref, k_ref, v_ref, lf_ref, *, width):
    xb = x_ref[...].astype(BF16)
    q_ref[...] = jnp.dot(xb, w_ref[:, 0:width], preferred_element_type=F32)
    k_ref[...] = jnp.dot(xb, w_ref[:, width:2 * width], preferred_element_type=F32)
    v_ref[...] = jnp.dot(xb, w_ref[:, 2 * width:3 * width], preferred_element_type=F32)
    z = jnp.dot(xb, wf_ref[...], preferred_element_type=F32) + bf_ref[...]
    lf_ref[...] = _log_sigmoid_pair(z)[0]


def _inproj_dec(x, w_qkv, w_f, b_f, lyr):
    n, d = x.shape
    width = w_qkv.shape[2] // 3
    full = lambda cols: pl.BlockSpec((n, cols), lambda i: (0, 0))
    return pl.pallas_call(
        functools.partial(_inproj_dec_kernel, width=width),
        grid=(1,),
        in_specs=[
            full(d),
            pl.BlockSpec((None, d, 3 * width), lambda i: (lyr, 0, 0)),
            pl.BlockSpec((None, d, GATE_COLS), lambda i: (lyr, 0, 0)),
            pl.BlockSpec((None, 1, GATE_COLS), lambda i: (lyr, 0, 0)),
        ],
        out_specs=[full(width), full(width), full(width), full(GATE_COLS)],
        out_shape=[jax.ShapeDtypeStruct((n, width), F32)] * 3
        + [jax.ShapeDtypeStruct((n, GATE_COLS), F32)],
        compiler_params=_cparams(1),
        name="inproj_dec",
    )(x, w_qkv, w_f, b_f)


def _outproj_kernel(x_ref, *refs, alpha, n_parts):
    o_refs = refs[:n_parts]
    w_ref, g_ref, b_ref, y_ref = refs[n_parts:]
    acc = None
    c0 = 0
    for o_ref in o_refs:
        cols = o_ref.shape[1]
        d = jnp.dot(o_ref[...], w_ref[c0:c0 + cols, :], preferred_element_type=F32)
        acc = d if acc is None else acc + d
        c0 += cols
    y_ref[...] = _layernorm(alpha * x_ref[...] + acc, g_ref[...], b_ref[...])


def _outproj_ln(x, o_parts, w_out, ln_g, ln_b, lyr, alpha, tm):
    n, d = x.shape
    assert n % tm == 0
    return pl.pallas_call(
        functools.partial(_outproj_kernel, alpha=alpha, n_parts=len(o_parts)),
        grid=(n // tm,),
        in_specs=[pl.BlockSpec((tm, d), lambda i: (i, 0))]
        + [pl.BlockSpec((tm, o.shape[1]), lambda i: (i, 0)) for o in o_parts]
        + [
            _resident((None, w_out.shape[1], d), lambda i: (lyr, 0, 0)),
            _resident((None, 1, d), lambda i: (lyr * 3 + 1, 0, 0)),
            _resident((None, 1, d), lambda i: (lyr * 3 + 1, 0, 0)),
        ],
        out_specs=pl.BlockSpec((tm, d), lambda i: (i, 0)),
        out_shape=jax.ShapeDtypeStruct((n, d), F32),
        compiler_params=_cparams(1),
        name="outproj_ln",
    )(x, *o_parts, w_out, ln_g, ln_b)


def _head_cols(h):
    return slice(h * HEAD_DIM, (h + 1) * HEAD_DIM)


def _kv_block(ref, j, cols):
    start = pl.multiple_of(j * ATT_BLOCK, ATT_BLOCK)
    return ref[0, pl.ds(start, ATT_BLOCK), cols]


def _softmax_step(s, vb, state, keep=None):
    m, l, acc = state
    if keep is not None:
        s = jnp.where(keep, s, NEG)
    m_new = jnp.maximum(m, jnp.max(s, axis=-1, keepdims=True))
    alpha = jnp.exp(m - m_new)
    p = jnp.exp(s - m_new)
    if keep is not None:
        p = jnp.where(keep, p, 0.0)
    l = alpha * l + jnp.sum(p, axis=-1, keepdims=True)
    acc = alpha * acc + jnp.dot(p.astype(BF16), vb, preferred_element_type=F32)
    return m_new, l, acc


def _softmax_init(rows, cols):
    return (jnp.full((rows, 1), NEG, F32), jnp.zeros((rows, 1), F32), jnp.zeros((rows, cols), F32))


def _block_iotas():
    r = lax.broadcasted_iota(jnp.int32, (ATT_BLOCK, ATT_BLOCK), 0)
    c = lax.broadcasted_iota(jnp.int32, (ATT_BLOCK, ATT_BLOCK), 1)
    return r, c


def _sb_kernel(q_ref, k_ref, v_ref, o_ref, *, scale):
    qi = pl.program_id(1)
    r, c = _block_iotas()
    suffix = (r > c).astype(BF16)
    before = c < r

    for h in range(HEADS_PER_MIXER):
        cols = _head_cols(h)
        qh = q_ref[0, :, cols]

        def step(j, carry, diagonal):
            later, acc = carry
            kb = _kv_block(k_ref, j, cols)
            vb = _kv_block(v_ref, j, cols)
            z = lax.dot_general(qh, kb, NT_DIMS, preferred_element_type=F32) * scale
            ls, l1m = _log_sigmoid_pair(z)
            if diagonal:
                l1m = jnp.where(before, l1m, 0.0)
            rest = _dot_exact_rhs(l1m, suffix, 2) + later
            w = jnp.exp(ls + rest)
            if diagonal:
                w = jnp.where(before, w, 0.0)
            acc = acc + jnp.dot(w.astype(BF16), vb, preferred_element_type=F32)
            later = later + jnp.sum(l1m, axis=-1, keepdims=True)
            return later, acc

        carry = (jnp.zeros((ATT_BLOCK, 1), F32), jnp.zeros((ATT_BLOCK, HEAD_DIM), F32))
        carry = step(qi, carry, True)
        carry = lax.fori_loop(0, qi, lambda jj, cr: step(qi - 1 - jj, cr, False), carry)
        o_ref[0, :, cols] = carry[1].astype(o_ref.dtype)


def _bias_tile(bias_ref, h, qi, j):
    return bias_ref[h, jnp.minimum(qi - j, 2)]


def _diff_kernel(q_ref, k_ref, v_ref, bias_ref, g_ref, dl_ref, o_ref, *, scale, lam_init):
    qi = pl.program_id(1)
    r, c = _block_iotas()
    causal = c <= r
    half = HEAD_DIM // 2
    dl = dl_ref[...]
    lam = (jnp.exp(jnp.sum(dl[0:1] * dl[1:2], axis=-1, keepdims=True))
           - jnp.exp(jnp.sum(dl[2:3] * dl[3:4], axis=-1, keepdims=True)) + lam_init)

    for h in range(HEADS_PER_MIXER):
        c1 = slice(h * HEAD_DIM, h * HEAD_DIM + half)
        c2 = slice(h * HEAD_DIM + half, (h + 1) * HEAD_DIM)
        cols = _head_cols(h)
        q1 = q_ref[0, :, c1]
        q2 = q_ref[0, :, c2]

        def step(j, carry, diagonal):
            st1, st2 = carry
            vb = _kv_block(v_ref, j, cols)
            bias = _bias_tile(bias_ref, h, qi, j)
            s1 = lax.dot_general(q1, _kv_block(k_ref, j, c1), NT_DIMS,
                                 preferred_element_type=F32) * scale + bias
            s2 = lax.dot_general(q2, _kv_block(k_ref, j, c2), NT_DIMS,
                                 preferred_element_type=F32) * scale + bias
            keep = causal if diagonal else None
            return _softmax_step(s1, vb, st1, keep), _softmax_step(s2, vb, st2, keep)

        carry = (_softmax_init(ATT_BLOCK, HEAD_DIM), _softmax_init(ATT_BLOCK, HEAD_DIM))
        carry = lax.fori_loop(0, qi, lambda j, cr: step(j, cr, False), carry)
        (_, l1, a1), (_, l2, a2) = step(qi, carry, True)
        o = a1 / l1 - lam * (a2 / l2)
        o = o * lax.rsqrt(jnp.mean(o * o, axis=-1, keepdims=True) + RMS_EPS)
        o = o * g_ref[...] * (1.0 - lam_init)
        o_ref[0, :, cols] = o.astype(o_ref.dtype)


def _fox_kernel(q_ref, k_ref, v_ref, cum_ref, cumt_ref, o_ref, *, scale):
    qi = pl.program_id(1)
    r, c = _block_iotas()
    causal = c <= r

    for h in range(HEADS_PER_MIXER):
        cols = _head_cols(h)
        qh = q_ref[0, :, cols]
        fq = cum_ref[0, :, h:h + 1]

        def step(j, state, diagonal):
            start = pl.multiple_of(j * ATT_BLOCK, ATT_BLOCK)
            fk = cumt_ref[0, h:h + 1, pl.ds(start, ATT_BLOCK)]
            s = lax.dot_general(qh, _kv_block(k_ref, j, cols), NT_DIMS,
                                preferred_element_type=F32) * scale + (fq - fk)
            return _softmax_step(s, _kv_block(v_ref, j, cols), state, causal if diagonal else None)

        state = _softmax_init(ATT_BLOCK, HEAD_DIM)
        state = lax.fori_loop(0, qi, lambda j, st: step(j, st, False), state)
        _, l, acc = step(qi, state, True)
        o_ref[0, :, cols] = (acc / l).astype(o_ref.dtype)


def _moba_kernel(q_ref, q32_ref, k_ref, v_ref, k32_ref, bias_ref, o_ref, kmean_ref, *, scale, n_blocks):
    qi = pl.program_id(1)
    r, c = _block_iotas()
    causal = c <= r
    lane = lax.broadcasted_iota(jnp.int32, (ATT_BLOCK, LANES), 1)

    @pl.when(qi == 0)
    def _():
        kmean_ref[...] = jnp.zeros_like(kmean_ref)
        for h in range(HEADS_PER_MIXER):
            for n in range(n_blocks):
                blk = k32_ref[0, n * MOBA_BLOCK:(n + 1) * MOBA_BLOCK, _head_cols(h)]
                kmean_ref[h, n:n + 1, :] = jnp.mean(blk, axis=0, keepdims=True)

    for h in range(HEADS_PER_MIXER):
        cols = _head_cols(h)
        qh = q_ref[0, :, cols]
        gates = _dot_f32(q32_ref[0, :, cols], kmean_ref[h], NT_DIMS)
        sel = _topk_select(gates, qi, n_blocks, lane)

        def scores(j):
            return (lax.dot_general(qh, _kv_block(k_ref, j, cols), NT_DIMS,
                                    preferred_element_type=F32) * scale
                    + _bias_tile(bias_ref, h, qi, j))

        state = _softmax_step(scores(qi), _kv_block(v_ref, qi, cols),
                              _softmax_init(ATT_BLOCK, HEAD_DIM), causal)

        def step(j, st):
            picked = jnp.sum(jnp.where(lane == j, sel, 0.0), axis=-1, keepdims=True) > 0.0
            return _softmax_step(scores(j), _kv_block(v_ref, j, cols), st, picked)

        _, l, acc = lax.fori_loop(0, qi, step, state)
        o_ref[0, :, cols] = (acc / l).astype(o_ref.dtype)


def _prompt_attention(qb, q32, kb, vb, k32, cum, cumt, bias_b, bias_d, subln_g, dlam, lam_init):
    bsz, seq, _ = qb.shape
    assert seq % ATT_BLOCK == 0
    nq = seq // ATT_BLOCK
    grid = (bsz, nq)

    def q_spec(m, cols=MIX_COLS):
        return pl.BlockSpec((1, ATT_BLOCK, cols), lambda b, i: (b, i, m))

    def kv_spec(m):
        return pl.BlockSpec((1, seq, MIX_COLS), lambda b, i: (b, 0, m))

    bias_spec = _resident((HEADS_PER_MIXER, 3, ATT_BLOCK, ATT_BLOCK), lambda b, i: (0, 0, 0, 0))
    out_spec = pl.BlockSpec((1, ATT_BLOCK, MIX_COLS), lambda b, i: (b, i, 0))
    out_shape = jax.ShapeDtypeStruct((bsz, seq, MIX_COLS), BF16)
    scale = HEAD_DIM ** -0.5

    o_a = pl.pallas_call(
        functools.partial(_sb_kernel, scale=scale),
        grid=grid, in_specs=[q_spec(0), kv_spec(0), kv_spec(0)],
        out_specs=out_spec, out_shape=out_shape, compiler_params=_cparams(2), name="attn_sb",
    )(qb, kb, vb)

    o_b = pl.pallas_call(
        functools.partial(_diff_kernel, scale=(HEAD_DIM // 2) ** -0.5, lam_init=lam_init),
        grid=grid,
        in_specs=[q_spec(1), kv_spec(1), kv_spec(1), bias_spec,
                  _resident((1, HEAD_DIM), lambda b, i: (0, 0)),
                  _resident(dlam.shape, lambda b, i: (0, 0))],
        out_specs=out_spec, out_shape=out_shape, compiler_params=_cparams(2), name="attn_diff",
    )(qb, kb, vb, bias_b, subln_g, dlam)

    o_c = pl.pallas_call(
        functools.partial(_fox_kernel, scale=scale),
        grid=grid,
        in_specs=[q_spec(2), kv_spec(2), kv_spec(2),
                  pl.BlockSpec((1, ATT_BLOCK, GATE_COLS), lambda b, i: (b, i, 0)),
                  pl.BlockSpec((1, 8, seq), lambda b, i: (b, 0, 0))],
        out_specs=out_spec, out_shape=out_shape, compiler_params=_cparams(2), name="attn_fox",
    )(qb, kb, vb, cum, cumt)

    o_d = pl.pallas_call(
        functools.partial(_moba_kernel, scale=scale, n_blocks=seq // MOBA_BLOCK),
        grid=grid,
        in_specs=[q_spec(3), q_spec(0), kv_spec(3), kv_spec(3), kv_spec(3), bias_spec],
        out_specs=out_spec, out_shape=out_shape,
        scratch_shapes=[pltpu.VMEM((HEADS_PER_MIXER, LANES, HEAD_DIM), F32)],
        compiler_params=_cparams(2), name="attn_moba",
    )(qb, q32, kb, vb, k32, bias_d)

    return [o.reshape(bsz * seq, MIX_COLS) for o in (o_a, o_b, o_c, o_d)]


ROW_FOX, ROW_SB, ROW_DIFF1, ROW_DIFF2, ROW_MOBA = 0, 4, 8, 12, 16


def _dec_kernel(pt_ref, q_ref, kn_ref, vn_ref, lfn_ref, ck_ref, cv_ref, clf_ref, rowmask_ref,
                outmask_ref, scale_ref, bias_ref, bself_ref, g_ref, dl_ref, o_ref,
                qb_sc, acc_sc, m_sc, l_sc, t_sc, ks_sc, *, n_pages, pages_per_block, lam_init):
    del pt_ref
    p = pl.program_id(1)
    rows = DEC_ROWS
    page = ck_ref.shape[1]
    rowi = lax.broadcasted_iota(jnp.int32, (rows, LANES), 0)
    is_fox = rowi < ROW_SB
    is_sb = (rowi >= ROW_SB) & (rowi < ROW_DIFF1)

    @pl.when(p == 0)
    def _():
        qb_sc[...] = (q_ref[0] * rowmask_ref[...]).astype(BF16)
        ks_sc[...] = jnp.zeros_like(ks_sc)

    kb = ck_ref[...].astype(BF16)
    vb = cv_ref[...].astype(BF16)
    s = jnp.dot(qb_sc[...], kb, preferred_element_type=F32)
    s = s * scale_ref[...] + bias_ref[...]
    ls, l1m = _log_sigmoid_pair(s)
    x8 = jnp.where(rowi[0:8] < ROW_SB, clf_ref[...], l1m[0:8])
    jr = lax.broadcasted_iota(jnp.int32, (page, page), 0)
    jc = lax.broadcasted_iota(jnp.int32, (page, page), 1)
    suffix = (jr > jc).astype(BF16)
    y8 = _dot_exact_rhs(x8, suffix, 3)
    t8 = jnp.sum(x8, axis=-1, keepdims=True)
    y = jnp.concatenate([y8, jnp.zeros((rows - 8, page), F32)], axis=0)
    s_soft = s + jnp.where(is_fox, y, 0.0)
    m = jnp.max(s_soft, axis=-1, keepdims=True)
    pr = jnp.exp(jnp.where(is_sb, ls + y, s_soft - m))
    l = jnp.sum(pr, axis=-1, keepdims=True)
    acc_sc[p] = lax.dot_general(pr.astype(BF16), vb, NT_DIMS, preferred_element_type=F32)
    m_sc[p] = jnp.broadcast_to(m, (rows, LANES))
    l_sc[p] = jnp.broadcast_to(l, (rows, LANES))
    t_sc[p] = jnp.broadcast_to(t8, (8, LANES))
    k_moba = ck_ref[(N_MIXERS - 1) * MIX_COLS:N_MIXERS * MIX_COLS, :]
    lane_m = lax.broadcasted_iota(jnp.int32, ks_sc.shape, 1)
    ks_sc[...] += jnp.where(lane_m == p // pages_per_block,
                            jnp.sum(k_moba, axis=-1, keepdims=True), 0.0)

    @pl.when(p == n_pages - 1)
    def _():
        q32 = q_ref[0] * rowmask_ref[...]
        s_self = (jnp.sum(q32 * kn_ref[0], axis=-1, keepdims=True) * scale_ref[...]
                  + bself_ref[...])

        n_blocks = n_pages // pages_per_block
        gates = _dot_f32(q32[:, (N_MIXERS - 1) * MIX_COLS:], ks_sc[...] * (1.0 / MOBA_BLOCK),
                         (((1,), (0,)), ((), ())))
        assert n_blocks <= LANES
        sel = _topk_select(gates, n_blocks, n_blocks,
                           lax.broadcasted_iota(jnp.int32, (rows, LANES), 1))
        is_moba = (rowi >= ROW_MOBA) & (rowi < ROW_MOBA + HEADS_PER_MIXER)

        run = jnp.where(rowi[0:8] < ROW_SB, lfn_ref[0], 0.0)
        later = [None] * n_pages
        for pp in reversed(range(n_pages)):
            later[pp] = jnp.concatenate([run, jnp.zeros((rows - 8, LANES), F32)], axis=0)
            run = run + t_sc[pp]

        def page_max(pp):
            mp = m_sc[pp] + jnp.where(is_fox, later[pp], 0.0)
            picked = sel[:, pp // pages_per_block:pp // pages_per_block + 1] > 0.0
            return jnp.where(is_moba & jnp.logical_not(picked), NEG, mp)

        big = s_self
        for pp in range(n_pages):
            big = jnp.maximum(big, page_max(pp))
        w_self = jnp.where(is_sb, 0.0, jnp.exp(s_self - big))
        tot = w_self
        acc = w_self[:, 0:1] * vn_ref[0]
        for pp in range(n_pages):
            w = jnp.exp(jnp.where(is_sb, later[pp], page_max(pp) - big))
            tot = tot + w * l_sc[pp]
            acc = acc + w[:, 0:1] * acc_sc[pp]
        acc = acc / jnp.where(is_sb, 1.0, tot)[:, 0:1]

        dl = dl_ref[...]
        lam = (jnp.exp(jnp.sum(dl[0:1] * dl[1:2], axis=-1, keepdims=True))
               - jnp.exp(jnp.sum(dl[2:3] * dl[3:4], axis=-1, keepdims=True)) + lam_init)
        is_diff2 = (rowi >= ROW_DIFF2) & (rowi < ROW_MOBA)
        coef = jnp.where(is_diff2, -lam, 1.0)[:, 0:1]
        out = jnp.sum(acc * outmask_ref[...] * coef, axis=0, keepdims=True)

        col = lax.broadcasted_iota(jnp.int32, out.shape, 1)
        sq = out * out
        factor = jnp.ones(out.shape, F32)
        for h in range(HEADS_PER_MIXER):
            lo = (HEADS_PER_MIXER + h) * HEAD_DIM
            in_head = (col >= lo) & (col < lo + HEAD_DIM)
            ms = jnp.sum(jnp.where(in_head, sq, 0.0), axis=-1, keepdims=True) * (1.0 / HEAD_DIM)
            factor = jnp.where(in_head, lax.rsqrt(ms + RMS_EPS) * (1.0 - lam_init), factor)
        is_diff_col = (col >= MIX_COLS) & (col < 2 * MIX_COLS)
        out = jnp.where(is_diff_col, out * factor * g_ref[...], out)
        o_ref[0] = out.astype(o_ref.dtype)


def _decode_attention(q, k_new, v_new, lf_new8, cache_k, cache_v, cache_lf8, page_table, lyr,
                      consts, bias_pages, bias_self, g_row, dlam, lam_init):
    n_seq, d = q.shape
    n_pages = page_table.shape[1]
    page = cache_k.shape[3]
    assert page == LANES and MOBA_BLOCK % page == 0 and (n_pages * page) % MOBA_BLOCK == 0
    rowmask, outmask, scale_rows = consts
    rows = DEC_ROWS
    seq3 = lambda a: a.reshape(n_seq, 1, a.shape[-1])

    def per_seq(cols):
        return pl.BlockSpec((1, 1, cols), lambda b, p, pt: (b, 0, 0))

    def const(shape):
        return pl.BlockSpec(shape, lambda b, p, pt: (0,) * len(shape))

    def paged(r, cols):
        return pl.BlockSpec((None, None, r, cols), lambda b, p, pt: (lyr, pt[b * n_pages + p], 0, 0))

    grid_spec = pltpu.PrefetchScalarGridSpec(
        num_scalar_prefetch=1,
        grid=(n_seq, n_pages),
        in_specs=[
            per_seq(d), per_seq(d), per_seq(d),
            pl.BlockSpec((1, 8, LANES), lambda b, p, pt: (b, 0, 0)),
            paged(d, page), paged(d, page), paged(8, page),
            const((rows, d)), const((rows, d)), const((rows, LANES)),
            pl.BlockSpec((None, rows, page), lambda b, p, pt: (p, 0, 0)),
            const((rows, LANES)), const((1, d)), const(dlam.shape),
        ],
        out_specs=per_seq(d),
        scratch_shapes=[
            pltpu.VMEM((rows, d), BF16),
            pltpu.VMEM((n_pages, rows, d), F32),
            pltpu.VMEM((n_pages, rows, LANES), F32),
            pltpu.VMEM((n_pages, rows, LANES), F32),
            pltpu.VMEM((n_pages, 8, LANES), F32),
            pltpu.VMEM((MIX_COLS, LANES), F32),
        ],
    )
    out = pl.pallas_call(
        functools.partial(_dec_kernel, n_pages=n_pages, pages_per_block=MOBA_BLOCK // page,
                          lam_init=lam_init),
        grid_spec=grid_spec,
        out_shape=jax.ShapeDtypeStruct((n_seq, 1, d), BF16),
        compiler_params=_cparams(2),
        name="attn_decode",
    )(page_table.reshape(-1), seq3(q), seq3(k_new), seq3(v_new), lf_new8, cache_k, cache_v,
      cache_lf8, rowmask, outmask, scale_rows, bias_pages, bias_self, g_row, dlam)
    return out.reshape(n_seq, d)


def _rel_bucket(dist):
    max_exact = N_BUCKETS // 2
    d = jnp.maximum(dist, 0)
    far = max_exact + (jnp.log(jnp.maximum(d, 1).astype(F32) / max_exact)
                       / math.log(MAX_DISTANCE / max_exact) * (N_BUCKETS - max_exact)).astype(jnp.int32)
    return jnp.where(d < max_exact, d, jnp.minimum(far, N_BUCKETS - 1))


def _prompt_bias_tiles(rel_bias_cols):
    assert ATT_BLOCK >= MAX_DISTANCE
    r = jnp.arange(ATT_BLOCK, dtype=jnp.int32)[:, None]
    c = jnp.arange(ATT_BLOCK, dtype=jnp.int32)[None, :]
    tiles = [rel_bias_cols[_rel_bucket(r - c + off * ATT_BLOCK)] for off in range(3)]
    return jnp.transpose(jnp.stack(tiles), (3, 0, 1, 2)).astype(F32)


def _row_heads():
    heads = [-1] * DEC_ROWS
    kind = ["pad"] * DEC_ROWS
    for i in range(HEADS_PER_MIXER):
        heads[ROW_SB + i], kind[ROW_SB + i] = 0 * HEADS_PER_MIXER + i, "full"
        heads[ROW_DIFF1 + i], kind[ROW_DIFF1 + i] = 1 * HEADS_PER_MIXER + i, "lo"
        heads[ROW_DIFF2 + i], kind[ROW_DIFF2 + i] = 1 * HEADS_PER_MIXER + i, "hi"
        heads[ROW_FOX + i], kind[ROW_FOX + i] = 2 * HEADS_PER_MIXER + i, "full"
        heads[ROW_MOBA + i], kind[ROW_MOBA + i] = 3 * HEADS_PER_MIXER + i, "full"
    return heads, kind


def _decode_consts(d):
    import numpy as np
    heads, kind = _row_heads()
    rowmask = np.zeros((DEC_ROWS, d), np.float32)
    outmask = np.zeros((DEC_ROWS, d), np.float32)
    scale = np.zeros((DEC_ROWS, LANES), np.float32)
    half = HEAD_DIM // 2
    for r in range(DEC_ROWS):
        if heads[r] < 0:
            continue
        lo = heads[r] * HEAD_DIM
        outmask[r, lo:lo + HEAD_DIM] = 1.0
        if kind[r] == "full":
            rowmask[r, lo:lo + HEAD_DIM] = 1.0
            scale[r] = HEAD_DIM ** -0.5
        elif kind[r] == "lo":
            rowmask[r, lo:lo + half] = 1.0
            scale[r] = half ** -0.5
        else:
            rowmask[r, lo + half:lo + HEAD_DIM] = 1.0
            scale[r] = half ** -0.5
    return jnp.asarray(rowmask), jnp.asarray(outmask), jnp.asarray(scale)


def _decode_bias(rel_bias, n_past, page):
    heads, _ = _row_heads()
    n_pages = n_past // page
    dist = n_past - jnp.arange(n_past, dtype=jnp.int32)
    per_key = rel_bias[_rel_bucket(dist)]
    self_b = rel_bias[_rel_bucket(jnp.zeros((), jnp.int32))]
    rows_k, rows_s = [], []
    for r in range(DEC_ROWS):
        h = heads[r]
        if h >= 3 * HEADS_PER_MIXER:
            c = HEADS_PER_MIXER + h - 3 * HEADS_PER_MIXER
        elif HEADS_PER_MIXER <= h < 2 * HEADS_PER_MIXER:
            c = h - HEADS_PER_MIXER
        else:
            c = None
        rows_k.append(jnp.zeros((n_past,), F32) if c is None else per_key[:, c])
        rows_s.append(jnp.zeros((), F32) if c is None else self_b[c])
    bias_keys = jnp.stack(rows_k).reshape(DEC_ROWS, n_pages, page).transpose(1, 0, 2)
    bias_self = jnp.broadcast_to(jnp.stack(rows_s)[:, None], (DEC_ROWS, LANES))
    return bias_keys.astype(F32), bias_self.astype(F32)


def _rows8(lf, n_heads):
    x = jnp.pad(lf[:, :n_heads], ((0, 0), (0, 8 - n_heads)))
    return jnp.broadcast_to(x[:, :, None], (x.shape[0], 8, LANES))


def kernel(x_prompt, x_sample, cache_k, cache_v, cache_logf, page_table, w_in, b_f, w_out,
           ffn_w_gu, ffn_w_down, ln_g, ln_b, rel_bias, diff_lambda, diff_subln_g):
    bsz, seq, d = x_prompt.shape
    n_dec, dec_seq, _ = x_sample.shape
    assert dec_seq == 1
    depth = w_in.shape[0]
    width = N_HEADS * HEAD_DIM
    n_pool, page = cache_k.shape[1], cache_k.shape[2]
    n_past = page_table.shape[1] * page
    alpha = (2 * depth) ** 0.25

    wgu = ffn_w_gu.astype(BF16)
    wdn = ffn_w_down.astype(BF16)
    w_qkv = w_in[:, :, :3 * width].astype(BF16)
    w_f = jnp.pad(w_in[:, :, 3 * width:], ((0, 0), (0, 0), (0, GATE_COLS - HEADS_PER_MIXER))).astype(BF16)
    b_f_row = jnp.pad(b_f, ((0, 0), (0, GATE_COLS - HEADS_PER_MIXER))).reshape(depth, 1, GATE_COLS)
    w_o = w_out.astype(BF16)
    ln_g3 = ln_g.reshape(depth * 3, 1, d)
    ln_b3 = ln_b.reshape(depth * 3, 1, d)

    bias_b = _prompt_bias_tiles(rel_bias[:, :HEADS_PER_MIXER])
    bias_d = _prompt_bias_tiles(rel_bias[:, HEADS_PER_MIXER:])
    dec_consts = _decode_consts(d)
    dec_bias, dec_bias_self = _decode_bias(rel_bias, n_past, page)
    cache_k4 = jnp.transpose(cache_k, (0, 1, 3, 4, 2)).reshape(depth, n_pool, width, page)
    cache_v4 = jnp.transpose(cache_v, (0, 1, 3, 4, 2)).reshape(depth, n_pool, width, page)
    cache_lf8 = jnp.pad(jnp.swapaxes(cache_logf, 2, 3), ((0, 0), (0, 0), (0, 8 - HEADS_PER_MIXER), (0, 0)))

    tm = 512
    xp = x_prompt.reshape(bsz * seq, d)
    xs = x_sample.reshape(n_dec, d)
    ks, vs, lfs, ksd, vsd, lfsd = [], [], [], [], [], []
    for lyr in range(depth):
        lam_init = 0.8 - 0.6 * math.exp(-0.3 * lyr)
        g_head = diff_subln_g[lyr].reshape(1, HEAD_DIM)
        g_row = jnp.tile(diff_subln_g[lyr], N_HEADS).reshape(1, d)
        dlam = diff_lambda[lyr]

        xp = _ffn_ln(xp, wgu, wdn, ln_g3, ln_b3, lyr, 0, alpha, tm)
        qb, q32, kf, vf, kb, vb, lf, cum, cumt = _inproj(xp, w_qkv, w_f, b_f_row, lyr, tm, seq)
        sh = lambda a: a.reshape(bsz, seq, a.shape[-1])
        o_parts = _prompt_attention(sh(qb), sh(q32), sh(kb), sh(vb), sh(kf), sh(cum), cumt,
                                    bias_b, bias_d, g_head, dlam, lam_init)
        xp = _outproj_ln(xp, o_parts, w_o, ln_g3, ln_b3, lyr, alpha, tm)
        xp = _ffn_ln(xp, wgu, wdn, ln_g3, ln_b3, lyr, 1, alpha, tm)
        ks.append(kf.reshape(bsz, seq, N_HEADS, HEAD_DIM))
        vs.append(vf.reshape(bsz, seq, N_HEADS, HEAD_DIM))
        lfs.append(lf[:, :HEADS_PER_MIXER].reshape(bsz, seq, HEADS_PER_MIXER))

        xs = _ffn_ln(xs, wgu, wdn, ln_g3, ln_b3, lyr, 0, alpha, n_dec)
        qd, kd, vd, lfd = _inproj_dec(xs, w_qkv, w_f, b_f_row, lyr)
        od = _decode_attention(qd, kd, vd, _rows8(lfd, HEADS_PER_MIXER), cache_k4, cache_v4, cache_lf8,
                               page_table, lyr, dec_consts, dec_bias, dec_bias_self, g_row, dlam,
                               lam_init)
        xs = _outproj_ln(xs, [od], w_o, ln_g3, ln_b3, lyr, alpha, n_dec)
        xs = _ffn_ln(xs, wgu, wdn, ln_g3, ln_b3, lyr, 1, alpha, n_dec)
        ksd.append(kd.reshape(n_dec, 1, N_HEADS, HEAD_DIM))
        vsd.append(vd.reshape(n_dec, 1, N_HEADS, HEAD_DIM))
        lfsd.append(lfd[:, :HEADS_PER_MIXER].reshape(n_dec, 1, HEADS_PER_MIXER))

    return (xp.reshape(bsz, seq, d), xs.reshape(n_dec, 1, d),
            jnp.stack(ks), jnp.stack(vs), jnp.stack(lfs),
            jnp.stack(ksd), jnp.stack(vsd), jnp.stack(lfsd))
```

```python
import functools
import math

import jax
import jax.numpy as jnp
from jax import lax
from jax.experimental import pallas as pl
from jax.experimental.pallas import tpu as pltpu

F32 = jnp.float32
BF16 = jnp.bfloat16

N_MIXERS = 4
HEADS_PER_MIXER = 4
N_HEADS = N_MIXERS * HEADS_PER_MIXER
HEAD_DIM = 64
MIX_COLS = HEADS_PER_MIXER * HEAD_DIM
N_BUCKETS = 32
MAX_DISTANCE = 128
MOBA_BLOCK = 256
MOBA_TOPK = 3
LN_EPS = 1e-5
RMS_EPS = 1e-5
NEG = -1e30

LANES = 128
GATE_COLS = LANES
VMEM_LIMIT = 56 * 2 ** 20
ATT_BLOCK = MOBA_BLOCK
DEC_ROWS = 32

NT_DIMS = (((1,), (1,)), ((), ()))


def _cparams(n_axes):
    return pltpu.CompilerParams(dimension_semantics=("arbitrary",) * n_axes,
                                vmem_limit_bytes=VMEM_LIMIT)


def _resident(block_shape, index_map):
    return pl.BlockSpec(block_shape, index_map, pipeline_mode=pl.Buffered(1))


def _layernorm(y, g, b):
    mu = jnp.mean(y, axis=-1, keepdims=True)
    d = y - mu
    var = jnp.mean(d * d, axis=-1, keepdims=True)
    return d * lax.rsqrt(var + LN_EPS) * g + b


def _log_sigmoid_pair(z):
    lg = jnp.log(1.0 + jnp.exp(-jnp.abs(z)))
    ls = jnp.minimum(z, 0.0) - lg
    return ls, ls - z


def _split_bf16(x, n):
    parts = []
    r = x
    for i in range(n):
        hi = r.astype(BF16)
        parts.append(hi)
        if i + 1 < n:
            r = r - hi.astype(F32)
    return parts


def _dot_exact_rhs(x, u, n):
    out = None
    for part in _split_bf16(x, n):
        d = jnp.dot(part, u, preferred_element_type=F32)
        out = d if out is None else out + d
    return out


def _dot_f32(a, b, dims):
    a1, a2, a3 = _split_bf16(a, 3)
    b1, b2, b3 = _split_bf16(b, 3)

    def d(x, y):
        return lax.dot_general(x, y, dims, preferred_element_type=F32)

    return d(a1, b1) + (d(a1, b2) + d(a2, b1)) + (d(a1, b3) + d(a2, b2) + d(a3, b1))


def _topk_select(gates, n_valid, n_max, block, axis=1):
    valid = block < n_valid
    gv = jnp.where(valid, gates, NEG)
    rank = jnp.zeros(gates.shape, F32)
    for m in range(n_max):
        cand = gv[:, m:m + 1] if axis == 1 else gv[m:m + 1, :]
        beats = (cand > gv) | ((cand == gv) & (m < block))
        rank = rank + beats.astype(F32)
    return jnp.where((rank < MOBA_TOPK) & valid, 1.0, 0.0)


def _ffn_kernel(x_ref, wgu_ref, wd_ref, g_ref, b_ref, o_ref, *, d_ff, chunk, alpha):
    x = x_ref[...]
    xb = x.astype(BF16)
    acc = jnp.zeros(x.shape, F32)
    for c in range(d_ff // chunk):
        gate = jnp.dot(xb, wgu_ref[:, c * chunk:(c + 1) * chunk], preferred_element_type=F32)
        up = jnp.dot(xb, wgu_ref[:, d_ff + c * chunk:d_ff + (c + 1) * chunk],
                     preferred_element_type=F32)
        h = (gate * jax.nn.sigmoid(gate) * up).astype(BF16)
        acc = acc + jnp.dot(h, wd_ref[c * chunk:(c + 1) * chunk, :], preferred_element_type=F32)
    o_ref[...] = _layernorm(alpha * x + 0.5 * acc, g_ref[...], b_ref[...])


def _ffn_ln(x, wgu, wd, ln_g, ln_b, lyr, idx, alpha, tm):
    n, d = x.shape
    d_ff = wd.shape[2]
    chunk = 256
    assert n % tm == 0 and d_ff % chunk == 0
    return pl.pallas_call(
        functools.partial(_ffn_kernel, d_ff=d_ff, chunk=chunk, alpha=alpha),
        grid=(n // tm,),
        in_specs=[
            pl.BlockSpec((tm, d), lambda i: (i, 0)),
            _resident((None, None, d, 2 * d_ff), lambda i: (lyr, idx, 0, 0)),
            _resident((None, None, d_ff, d), lambda i: (lyr, idx, 0, 0)),
            _resident((None, 1, d), lambda i: (lyr * 3 + 2 * idx, 0, 0)),
            _resident((None, 1, d), lambda i: (lyr * 3 + 2 * idx, 0, 0)),
        ],
        out_specs=pl.BlockSpec((tm, d), lambda i: (i, 0)),
        out_shape=jax.ShapeDtypeStruct((n, d), F32),
        compiler_params=_cparams(1),
        name="ffn_ln",
    )(x, wgu, wd, ln_g, ln_b)


def _inproj_kernel(x_ref, w_ref, wf_ref, bf_ref, qb_ref, q32_ref, kf_ref, vf_ref, kb_ref, vb_ref,
                   lf_ref, cum_ref, cumt_ref, carry_ref, *, width, tiles_per_seq, q_cols):
    i = pl.program_id(0)
    xb = x_ref[...].astype(BF16)
    q = jnp.dot(xb, w_ref[:, 0:width], preferred_element_type=F32)
    qb_ref[...] = q.astype(BF16)
    q32_ref[...] = q[:, width - q_cols:]
    k = jnp.dot(xb, w_ref[:, width:2 * width], preferred_element_type=F32)
    kf_ref[...] = k
    kb_ref[...] = k.astype(BF16)
    v = jnp.dot(xb, w_ref[:, 2 * width:3 * width], preferred_element_type=F32)
    vf_ref[...] = v
    vb_ref[...] = v.astype(BF16)
    z = jnp.dot(xb, wf_ref[...], preferred_element_type=F32) + bf_ref[...]
    lf, _ = _log_sigmoid_pair(z)
    lf_ref[...] = lf

    @pl.when(i % tiles_per_seq == 0)
    def _():
        carry_ref[...] = jnp.zeros_like(carry_ref)

    tm = lf.shape[0]
    row = lax.broadcasted_iota(jnp.int32, (tm, tm), 0)
    col = lax.broadcasted_iota(jnp.int32, (tm, tm), 1)
    lower = (col <= row).astype(BF16)
    cum = carry_ref[0:1, :]
    for part in _split_bf16(lf, 3):
        cum = cum + jnp.dot(lower, part, preferred_element_type=F32)
    carry_ref[...] = jnp.broadcast_to(cum[tm - 1:tm, :], carry_ref.shape)
    cum_ref[...] = cum
    cumt_ref[0] = cum.T[0:8, :]


def _inproj(x, w_qkv, w_f, b_f, lyr, tm, seq_len):
    n, d = x.shape
    width = w_qkv.shape[2] // 3
    assert n % tm == 0 and seq_len % tm == 0
    tiles_per_seq = seq_len // tm
    n_seq = n // seq_len
    row_spec = lambda cols: pl.BlockSpec((tm, cols), lambda i: (i, 0))
    outs = pl.pallas_call(
        functools.partial(_inproj_kernel, width=width, tiles_per_seq=tiles_per_seq, q_cols=MIX_COLS),
        grid=(n // tm,),
        in_specs=[
            pl.BlockSpec((tm, d), lambda i: (i, 0)),
            _resident((None, d, 3 * width), lambda i: (lyr, 0, 0)),
            _resident((None, d, GATE_COLS), lambda i: (lyr, 0, 0)),
            _resident((None, 1, GATE_COLS), lambda i: (lyr, 0, 0)),
        ],
        out_specs=[
            row_spec(width), row_spec(MIX_COLS), row_spec(width), row_spec(width),
            row_spec(width), row_spec(width), row_spec(GATE_COLS), row_spec(GATE_COLS),
            pl.BlockSpec((1, 8, tm), lambda i: (i // tiles_per_seq, 0, i % tiles_per_seq)),
        ],
        out_shape=[
            jax.ShapeDtypeStruct((n, width), BF16),
            jax.ShapeDtypeStruct((n, MIX_COLS), F32),
            jax.ShapeDtypeStruct((n, width), F32),
            jax.ShapeDtypeStruct((n, width), F32),
            jax.ShapeDtypeStruct((n, width), BF16),
            jax.ShapeDtypeStruct((n, width), BF16),
            jax.ShapeDtypeStruct((n, GATE_COLS), F32),
            jax.ShapeDtypeStruct((n, GATE_COLS), F32),
            jax.ShapeDtypeStruct((n_seq, 8, seq_len), F32),
        ],
        scratch_shapes=[pltpu.VMEM((8, GATE_COLS), F32)],
        compiler_params=_cparams(1),
        name="inproj",
    )(x, w_qkv, w_f, b_f)
    return outs


def _inproj_dec_kernel(x_ref, w_ref, wf_ref, bf_ref, q_ref, k_ref, v_ref, lf_ref, *, width):
    xb = x_ref[...].astype(BF16)
    q_ref[...] = jnp.dot(xb, w_ref[:, 0:width], preferred_element_type=F32)
    k_ref[...] = jnp.dot(xb, w_ref[:, width:2 * width], preferred_element_type=F32)
    v_ref[...] = jnp.dot(xb, w_ref[:, 2 * width:3 * width], preferred_element_type=F32)
    z = jnp.dot(xb, wf_ref[...], preferred_element_type=F32) + bf_ref[...]
    lf_ref[...] = _log_sigmoid_pair(z)[0]


def _inproj_dec(x, w_qkv, w_f, b_f, lyr):
    n, d = x.shape
    width = w_qkv.shape[2] // 3
    full = lambda cols: pl.BlockSpec((n, cols), lambda i: (0, 0))
    return pl.pallas_call(
        functools.partial(_inproj_dec_kernel, width=width),
        grid=(1,),
        in_specs=[
            full(d),
            pl.BlockSpec((None, d, 3 * width), lambda i: (lyr, 0, 0)),
            pl.BlockSpec((None, d, GATE_COLS), lambda i: (lyr, 0, 0)),
            pl.BlockSpec((None, 1, GATE_COLS), lambda i: (lyr, 0, 0)),
        ],
        out_specs=[full(width), full(width), full(width), full(GATE_COLS)],
        out_shape=[jax.ShapeDtypeStruct((n, width), F32)] * 3
        + [jax.ShapeDtypeStruct((n, GATE_COLS), F32)],
        compiler_params=_cparams(1),
        name="inproj_dec",
    )(x, w_qkv, w_f, b_f)


def _outproj_kernel(x_ref, *refs, alpha, n_parts):
    o_refs = refs[:n_parts]
    w_ref, g_ref, b_ref, y_ref = refs[n_parts:]
    acc = None
    c0 = 0
    for o_ref in o_refs:
        cols = o_ref.shape[1]
        d = jnp.dot(o_ref[...], w_ref[c0:c0 + cols, :], preferred_element_type=F32)
        acc = d if acc is None else acc + d
        c0 += cols
    y_ref[...] = _layernorm(alpha * x_ref[...] + acc, g_ref[...], b_ref[...])


def _outproj_ln(x, o_parts, w_out, ln_g, ln_b, lyr, alpha, tm):
    n, d = x.shape
    assert n % tm == 0
    return pl.pallas_call(
        functools.partial(_outproj_kernel, alpha=alpha, n_parts=len(o_parts)),
        grid=(n // tm,),
        in_specs=[pl.BlockSpec((tm, d), lambda i: (i, 0))]
        + [pl.BlockSpec((tm, o.shape[1]), lambda i: (i, 0)) for o in o_parts]
        + [
            _resident((None, w_out.shape[1], d), lambda i: (lyr, 0, 0)),
            _resident((None, 1, d), lambda i: (lyr * 3 + 1, 0, 0)),
            _resident((None, 1, d), lambda i: (lyr * 3 + 1, 0, 0)),
        ],
        out_specs=pl.BlockSpec((tm, d), lambda i: (i, 0)),
        out_shape=jax.ShapeDtypeStruct((n, d), F32),
        compiler_params=_cparams(1),
        name="outproj_ln",
    )(x, *o_parts, w_out, ln_g, ln_b)


_HEADS = range(HEADS_PER_MIXER)


def _head_cols(h):
    return slice(h * HEAD_DIM, (h + 1) * HEAD_DIM)


def _kv_block(ref, j, cols):
    start = pl.multiple_of(j * ATT_BLOCK, ATT_BLOCK)
    return ref[0, pl.ds(start, ATT_BLOCK), cols]


def _softmax_step(s, vb, state, keep=None):
    m, l, acc = state
    if keep is not None:
        s = jnp.where(keep, s, NEG)
    m_new = jnp.maximum(m, jnp.max(s, axis=-1, keepdims=True))
    alpha = jnp.exp(m - m_new)
    p = jnp.exp(s - m_new)
    if keep is not None:
        p = jnp.where(keep, p, 0.0)
    l = alpha * l + jnp.sum(p, axis=-1, keepdims=True)
    acc = alpha * acc + jnp.dot(p.astype(BF16), vb, preferred_element_type=F32)
    return m_new, l, acc


def _softmax_init(rows, cols):
    return (jnp.full((rows, 1), NEG, F32), jnp.zeros((rows, 1), F32), jnp.zeros((rows, cols), F32))


def _block_iotas():
    r = lax.broadcasted_iota(jnp.int32, (ATT_BLOCK, ATT_BLOCK), 0)
    c = lax.broadcasted_iota(jnp.int32, (ATT_BLOCK, ATT_BLOCK), 1)
    return r, c


ROW_TILE = ATT_BLOCK
_TILES = range(ATT_BLOCK // ROW_TILE)


def _tile_rows(t):
    return slice(t * ROW_TILE, (t + 1) * ROW_TILE)


def _tile_iotas(t):
    r = lax.broadcasted_iota(jnp.int32, (ROW_TILE, ATT_BLOCK), 0) + t * ROW_TILE
    c = lax.broadcasted_iota(jnp.int32, (ROW_TILE, ATT_BLOCK), 1)
    return r, c


def _load_q_tiles(q_ref, scale=None):
    def load(h, t):
        q = q_ref[0, _tile_rows(t), _head_cols(h)]
        return q if scale is None else q * scale
    return [[load(h, t) for t in _TILES] for h in _HEADS]


def _store_head(o_ref, h, tiles):
    o_ref[0, :, _head_cols(h)] = jnp.concatenate(tiles, axis=0).astype(o_ref.dtype)


def _walk_blocks(qi, init, load, phases, diagonal_first=False, descending=False):
    def step(j, carry, diagonal):
        blocks = [load(h, j) for h in _HEADS]
        vals = [[carry[h][t] for t in _TILES] for h in _HEADS]
        for phase in phases:
            vals = [[phase(h, t, blocks[h], vals[h][t], diagonal) for t in _TILES] for h in _HEADS]
        return tuple(tuple(v) for v in vals)

    carry = tuple(tuple(init() for _ in _TILES) for _ in _HEADS)
    if diagonal_first:
        carry = step(qi, carry, True)
    if descending:
        carry = lax.fori_loop(0, qi, lambda jj, cr: step(qi - 1 - jj, cr, False), carry)
    else:
        carry = lax.fori_loop(0, qi, lambda j, cr: step(j, cr, False), carry)
    if not diagonal_first:
        carry = step(qi, carry, True)
    return carry


def _sb_kernel(q_ref, k_ref, v_ref, o_ref, *, scale):
    qi = pl.program_id(1)
    r, c = _block_iotas()
    suffix = (r > c).astype(BF16)

    qs = _load_q_tiles(q_ref, scale)

    def load(h, j):
        return _kv_block(k_ref, j, _head_cols(h)), _kv_block(v_ref, j, _head_cols(h))

    def scores(h, t, blocks, state, diagonal):
        return state, lax.dot_general(qs[h][t], blocks[0], NT_DIMS, preferred_element_type=F32)

    def log_terms(h, t, blocks, value, diagonal):
        state, z = value
        ls, l1m = _log_sigmoid_pair(z)
        if diagonal:
            rt, ct = _tile_iotas(t)
            l1m = jnp.where(ct < rt, l1m, 0.0)
        return state, ls, l1m[:, 0:1], _dot_exact_rhs(l1m, suffix, 2)

    def weigh(h, t, blocks, value, diagonal):
        (later, acc), ls, l1m_first, rest_local = value
        w = jnp.exp(ls + (rest_local + later))
        if diagonal:
            rt, ct = _tile_iotas(t)
            w = jnp.where(ct < rt, w, 0.0)
        acc = acc + jnp.dot(w.astype(BF16), blocks[1], preferred_element_type=F32)
        return later + (rest_local[:, 0:1] + l1m_first), acc

    def init():
        return jnp.zeros((ROW_TILE, 1), F32), jnp.zeros((ROW_TILE, HEAD_DIM), F32)

    carry = _walk_blocks(qi, init, load, [scores, log_terms, weigh], diagonal_first=True,
                         descending=True)
    for h in _HEADS:
        _store_head(o_ref, h, [carry[h][t][1] for t in _TILES])


def _diff_kernel(q_ref, k_ref, v_ref, bias_ref, g_ref, dl_ref, o_ref, *, scale, lam_init):
    qi = pl.program_id(1)
    half = HEAD_DIM // 2
    dl = dl_ref[...]
    lam = (jnp.exp(jnp.sum(dl[0:1] * dl[1:2], axis=-1, keepdims=True))
           - jnp.exp(jnp.sum(dl[2:3] * dl[3:4], axis=-1, keepdims=True)) + lam_init)

    def half_cols(h, i):
        return slice(h * HEAD_DIM + i * half, h * HEAD_DIM + (i + 1) * half)

    qs = [[[q_ref[0, _tile_rows(t), half_cols(h, i)] for i in range(2)] for t in _TILES]
          for h in _HEADS]

    def load(h, j):
        return ([_kv_block(k_ref, j, half_cols(h, i)) for i in range(2)],
                _kv_block(v_ref, j, _head_cols(h)), jnp.minimum(qi - j, 2))

    def scores(h, t, blocks, state, diagonal):
        return state, [lax.dot_general(qs[h][t][i], blocks[0][i], NT_DIMS, preferred_element_type=F32)
                       for i in range(2)]

    def attend(h, t, blocks, value, diagonal):
        state, raw = value
        _, vb, kind = blocks
        bias = bias_ref[h, kind, _tile_rows(t), :]
        keep = None
        if diagonal:
            rt, ct = _tile_iotas(t)
            keep = ct <= rt
        return tuple(_softmax_step(raw[i] * scale + bias, vb, state[i], keep) for i in range(2))

    def init():
        return _softmax_init(ROW_TILE, HEAD_DIM), _softmax_init(ROW_TILE, HEAD_DIM)

    carry = _walk_blocks(qi, init, load, [scores, attend])
    for h in _HEADS:
        tiles = []
        for t in _TILES:
            (_, l1, a1), (_, l2, a2) = carry[h][t]
            o = a1 / l1 - lam * (a2 / l2)
            o = o * lax.rsqrt(jnp.mean(o * o, axis=-1, keepdims=True) + RMS_EPS)
            tiles.append(o * g_ref[...] * (1.0 - lam_init))
        _store_head(o_ref, h, tiles)


def _fox_kernel(q_ref, k_ref, v_ref, cum_ref, cumt_ref, o_ref, *, scale):
    qi = pl.program_id(1)
    qs = _load_q_tiles(q_ref, scale)
    fqs = [[jnp.broadcast_to(cum_ref[0, _tile_rows(t), h:h + 1], (ROW_TILE, ATT_BLOCK))
            for t in _TILES] for h in _HEADS]

    def load(h, j):
        start = pl.multiple_of(j * ATT_BLOCK, ATT_BLOCK)
        return (_kv_block(k_ref, j, _head_cols(h)), _kv_block(v_ref, j, _head_cols(h)),
                cumt_ref[0, h:h + 1, pl.ds(start, ATT_BLOCK)])

    def scores(h, t, blocks, state, diagonal):
        return state, lax.dot_general(qs[h][t], blocks[0], NT_DIMS, preferred_element_type=F32)

    def attend(h, t, blocks, value, diagonal):
        state, raw = value
        _, vb, fk = blocks
        keep = None
        if diagonal:
            rt, ct = _tile_iotas(t)
            keep = ct <= rt
        return _softmax_step(raw + (fqs[h][t] - fk), vb, state, keep)

    carry = _walk_blocks(qi, lambda: _softmax_init(ROW_TILE, HEAD_DIM), load, [scores, attend])
    for h in _HEADS:
        _store_head(o_ref, h, [carry[h][t][2] / carry[h][t][1] for t in _TILES])


def _moba_kernel(q_ref, q32_ref, k_ref, v_ref, k32_ref, bias_ref, o_ref, kmean_ref, *, scale, n_blocks):
    qi = pl.program_id(1)
    lane = lax.broadcasted_iota(jnp.int32, (ROW_TILE, LANES), 1)

    @pl.when(qi == 0)
    def _():
        kmean_ref[...] = jnp.zeros_like(kmean_ref)
        for h in range(HEADS_PER_MIXER):
            for n in range(n_blocks):
                blk = k32_ref[0, n * MOBA_BLOCK:(n + 1) * MOBA_BLOCK, _head_cols(h)]
                kmean_ref[h, n:n + 1, :] = jnp.mean(blk, axis=0, keepdims=True)

    qs = _load_q_tiles(q_ref, scale)
    nb8 = -(-n_blocks // 8) * 8
    assert nb8 <= LANES
    blk = lax.broadcasted_iota(jnp.int32, (nb8, ATT_BLOCK), 0)

    def select(h):
        gates_t = _dot_f32(kmean_ref[h, 0:nb8, :], q32_ref[0, :, _head_cols(h)], NT_DIMS)
        sel_t = _topk_select(gates_t, qi, n_blocks, blk, axis=0)
        sel = jnp.concatenate([sel_t, jnp.zeros((LANES - nb8, ATT_BLOCK), F32)], axis=0).T
        return [sel[_tile_rows(t)] for t in _TILES]

    sels = [select(h) for h in _HEADS]

    def load(h, j):
        return (_kv_block(k_ref, j, _head_cols(h)), _kv_block(v_ref, j, _head_cols(h)),
                jnp.minimum(qi - j, 2), j)

    def scores(h, t, blocks, state, diagonal):
        return state, lax.dot_general(qs[h][t], blocks[0], NT_DIMS, preferred_element_type=F32)

    def attend(h, t, blocks, value, diagonal):
        state, raw = value
        _, vb, kind, j = blocks
        if diagonal:
            rt, ct = _tile_iotas(t)
            keep = ct <= rt
        else:
            keep = jnp.sum(jnp.where(lane == j, sels[h][t], 0.0), axis=-1, keepdims=True) > 0.0
        return _softmax_step(raw + bias_ref[h, kind, _tile_rows(t), :], vb, state, keep)

    carry = _walk_blocks(qi, lambda: _softmax_init(ROW_TILE, HEAD_DIM), load, [scores, attend],
                         diagonal_first=True)
    for h in _HEADS:
        _store_head(o_ref, h, [carry[h][t][2] / carry[h][t][1] for t in _TILES])


def _prompt_attention(qb, q32, kb, vb, k32, cum, cumt, bias_b, bias_d, subln_g, dlam, lam_init):
    bsz, seq, _ = qb.shape
    assert seq % ATT_BLOCK == 0
    nq = seq // ATT_BLOCK
    grid = (bsz, nq)

    def q_spec(m, cols=MIX_COLS):
        return pl.BlockSpec((1, ATT_BLOCK, cols), lambda b, i: (b, i, m))

    def kv_spec(m):
        return pl.BlockSpec((1, seq, MIX_COLS), lambda b, i: (b, 0, m))

    bias_spec = _resident((HEADS_PER_MIXER, 3, ATT_BLOCK, ATT_BLOCK), lambda b, i: (0, 0, 0, 0))
    out_spec = pl.BlockSpec((1, ATT_BLOCK, MIX_COLS), lambda b, i: (b, i, 0))
    out_shape = jax.ShapeDtypeStruct((bsz, seq, MIX_COLS), BF16)
    scale = HEAD_DIM ** -0.5

    o_a = pl.pallas_call(
        functools.partial(_sb_kernel, scale=scale),
        grid=grid, in_specs=[q_spec(0), kv_spec(0), kv_spec(0)],
        out_specs=out_spec, out_shape=out_shape, compiler_params=_cparams(2), name="attn_sb",
    )(qb, kb, vb)

    o_b = pl.pallas_call(
        functools.partial(_diff_kernel, scale=(HEAD_DIM // 2) ** -0.5, lam_init=lam_init),
        grid=grid,
        in_specs=[q_spec(1), kv_spec(1), kv_spec(1), bias_spec,
                  _resident((1, HEAD_DIM), lambda b, i: (0, 0)),
                  _resident(dlam.shape, lambda b, i: (0, 0))],
        out_specs=out_spec, out_shape=out_shape, compiler_params=_cparams(2), name="attn_diff",
    )(qb, kb, vb, bias_b, subln_g, dlam)

    o_c = pl.pallas_call(
        functools.partial(_fox_kernel, scale=scale),
        grid=grid,
        in_specs=[q_spec(2), kv_spec(2), kv_spec(2),
                  pl.BlockSpec((1, ATT_BLOCK, GATE_COLS), lambda b, i: (b, i, 0)),
                  pl.BlockSpec((1, 8, seq), lambda b, i: (b, 0, 0))],
        out_specs=out_spec, out_shape=out_shape, compiler_params=_cparams(2), name="attn_fox",
    )(qb, kb, vb, cum, cumt)

    o_d = pl.pallas_call(
        functools.partial(_moba_kernel, scale=scale, n_blocks=seq // MOBA_BLOCK),
        grid=grid,
        in_specs=[q_spec(3), q_spec(0), kv_spec(3), kv_spec(3), kv_spec(3), bias_spec],
        out_specs=out_spec, out_shape=out_shape,
        scratch_shapes=[pltpu.VMEM((HEADS_PER_MIXER, LANES, HEAD_DIM), F32)],
        compiler_params=_cparams(2), name="attn_moba",
    )(qb, q32, kb, vb, k32, bias_d)

    return [o.reshape(bsz * seq, MIX_COLS) for o in (o_a, o_b, o_c, o_d)]


ROW_FOX, ROW_SB, ROW_DIFF1, ROW_DIFF2, ROW_MOBA = 0, 4, 8, 12, 16


def _dec_seq_kernel(pt_ref, q_ref, kn_ref, vn_ref, lfn_ref, *refs, n_pages, pages_per_block, lam_init):
    del pt_ref
    k_refs = refs[:n_pages]
    v_refs = refs[n_pages:2 * n_pages]
    lf_refs = refs[2 * n_pages:3 * n_pages]
    rowmask_ref, outmask_ref, scale_ref, bias_ref, bself_ref, g_ref, dl_ref, o_ref = refs[3 * n_pages:]
    rows = DEC_ROWS
    page = k_refs[0].shape[1]
    group = 2 if n_pages % 2 == 0 else 1
    rowi = lax.broadcasted_iota(jnp.int32, (rows, LANES), 0)
    lane = lax.broadcasted_iota(jnp.int32, (rows, LANES), 1)
    row8 = rowi[0:8]
    is_sb = (rowi >= ROW_SB) & (rowi < ROW_DIFF1)
    is_moba = (rowi >= ROW_MOBA) & (rowi < ROW_MOBA + HEADS_PER_MIXER)
    is_diff2 = (rowi >= ROW_DIFF2) & (rowi < ROW_MOBA)

    q32 = q_ref[0] * rowmask_ref[...]
    qb = q32.astype(BF16)
    scale = scale_ref[...]

    s_pages = []
    for g0 in range(0, n_pages, group):
        kb = jnp.concatenate([k_refs[g][...].astype(BF16) for g in range(g0, g0 + group)], axis=1)
        sg = jnp.dot(qb, kb, preferred_element_type=F32)
        s_pages += [sg[:, i * page:(i + 1) * page] for i in range(group)]
    s_pages = [s_pages[p] * scale + bias_ref[p] for p in range(n_pages)]

    ls8, x8 = [], []
    for p in range(n_pages):
        ls, l1m = _log_sigmoid_pair(s_pages[p][0:8])
        ls8.append(ls)
        x8.append(jnp.where(row8 < ROW_SB, lf_refs[p][...], l1m))
    x_all = jnp.concatenate(x8, axis=0)
    jr = lax.broadcasted_iota(jnp.int32, (page, page), 0)
    jc = lax.broadcasted_iota(jnp.int32, (page, page), 1)
    y_all = _dot_exact_rhs(x_all, (jr > jc).astype(BF16), 3)
    t_all = jnp.sum(x_all, axis=-1, keepdims=True)

    run = jnp.where(row8 < ROW_SB, lfn_ref[0], 0.0)
    adj8 = [None] * n_pages
    for p in reversed(range(n_pages)):
        base = jnp.where(row8 < ROW_SB, s_pages[p][0:8], ls8[p])
        adj8[p] = base + (y_all[8 * p:8 * p + 8] + run)
        run = run + t_all[8 * p:8 * p + 8]

    n_blocks = n_pages // pages_per_block
    assert n_blocks <= LANES
    moba_rows = slice((N_MIXERS - 1) * MIX_COLS, N_MIXERS * MIX_COLS)
    lane_m = lax.broadcasted_iota(jnp.int32, (MIX_COLS, LANES), 1)
    ksum = jnp.zeros((MIX_COLS, LANES), F32)
    for n in range(n_blocks):
        col = None
        for p in range(n * pages_per_block, (n + 1) * pages_per_block):
            cs = jnp.sum(k_refs[p][moba_rows, :], axis=-1, keepdims=True)
            col = cs if col is None else col + cs
        ksum = jnp.where(lane_m == n, col, ksum)
    gates = _dot_f32(q32[:, moba_rows], ksum * (1.0 / MOBA_BLOCK), (((1,), (0,)), ((), ())))
    sel = _topk_select(gates, n_blocks, n_blocks, lane)

    s_eff = []
    for p in range(n_pages):
        full = jnp.concatenate([adj8[p], s_pages[p][8:]], axis=0)
        n = p // pages_per_block
        dropped = is_moba & jnp.logical_not(sel[:, n:n + 1] > 0.0)
        s_eff.append(jnp.where(dropped, NEG, full))
    s_self = (jnp.sum(q32 * kn_ref[0], axis=-1, keepdims=True) * scale + bself_ref[...])
    mx = s_eff[0]
    for p in range(1, n_pages):
        mx = jnp.maximum(mx, s_eff[p])
    big = jnp.maximum(jnp.max(mx, axis=-1, keepdims=True), s_self)
    probs = [jnp.exp(jnp.where(is_sb, s_eff[p], s_eff[p] - big)) for p in range(n_pages)]
    w_self = jnp.where(is_sb, 0.0, jnp.exp(s_self - big))
    psum = probs[0]
    for p in range(1, n_pages):
        psum = psum + probs[p]
    tot = jnp.sum(psum, axis=-1, keepdims=True) + w_self

    acc = w_self[:, 0:1] * vn_ref[0]
    for g0 in range(0, n_pages, group):
        pg = jnp.concatenate([probs[g] for g in range(g0, g0 + group)], axis=1).astype(BF16)
        vb = jnp.concatenate([v_refs[g][...].astype(BF16) for g in range(g0, g0 + group)], axis=1)
        acc = acc + lax.dot_general(pg, vb, NT_DIMS, preferred_element_type=F32)
    acc = acc / jnp.where(is_sb, 1.0, tot)[:, 0:1]

    dl = dl_ref[...]
    lam = (jnp.exp(jnp.sum(dl[0:1] * dl[1:2], axis=-1, keepdims=True))
           - jnp.exp(jnp.sum(dl[2:3] * dl[3:4], axis=-1, keepdims=True)) + lam_init)
    coef = jnp.where(is_diff2, -lam, 1.0)[:, 0:1]
    out = jnp.sum(acc * outmask_ref[...] * coef, axis=0, keepdims=True)

    col = lax.broadcasted_iota(jnp.int32, out.shape, 1)
    sq = out * out
    factor = jnp.ones(out.shape, F32)
    for h in _HEADS:
        lo = (HEADS_PER_MIXER + h) * HEAD_DIM
        in_head = (col >= lo) & (col < lo + HEAD_DIM)
        ms = jnp.sum(jnp.where(in_head, sq, 0.0), axis=-1, keepdims=True) * (1.0 / HEAD_DIM)
        factor = jnp.where(in_head, lax.rsqrt(ms + RMS_EPS) * (1.0 - lam_init), factor)
    is_diff_col = (col >= MIX_COLS) & (col < 2 * MIX_COLS)
    out = jnp.where(is_diff_col, out * factor * g_ref[...], out)
    o_ref[0] = out.astype(o_ref.dtype)


def _decode_attention(q, k_new, v_new, lf_new8, cache_k, cache_v, cache_lf8, page_table, lyr,
                      consts, bias_pages, bias_self, g_row, dlam, lam_init):
    n_seq, d = q.shape
    n_pages = page_table.shape[1]
    page = cache_k.shape[3]
    assert page == LANES and MOBA_BLOCK % page == 0 and (n_pages * page) % MOBA_BLOCK == 0
    rowmask, outmask, scale_rows = consts
    rows = DEC_ROWS
    seq3 = lambda a: a.reshape(n_seq, 1, a.shape[-1])

    def per_seq(cols):
        return pl.BlockSpec((1, 1, cols), lambda b, pt: (b, 0, 0))

    def const(shape):
        return pl.BlockSpec(shape, lambda b, pt: (0,) * len(shape))

    def paged(r, cols, p):
        return pl.BlockSpec((None, None, r, cols), lambda b, pt: (lyr, pt[b * n_pages + p], 0, 0))

    grid_spec = pltpu.PrefetchScalarGridSpec(
        num_scalar_prefetch=1,
        grid=(n_seq,),
        in_specs=[per_seq(d), per_seq(d), per_seq(d),
                  pl.BlockSpec((1, 8, LANES), lambda b, pt: (b, 0, 0))]
        + [paged(d, page, p) for p in range(n_pages)]
        + [paged(d, page, p) for p in range(n_pages)]
        + [paged(8, page, p) for p in range(n_pages)]
        + [const((rows, d)), const((rows, d)), const((rows, LANES)), const((n_pages, rows, page)),
           const((rows, LANES)), const((1, d)), const(dlam.shape)],
        out_specs=per_seq(d),
    )
    out = pl.pallas_call(
        functools.partial(_dec_seq_kernel, n_pages=n_pages, pages_per_block=MOBA_BLOCK // page,
                          lam_init=lam_init),
        grid_spec=grid_spec,
        out_shape=jax.ShapeDtypeStruct((n_seq, 1, d), BF16),
        compiler_params=_cparams(1),
        name="attn_decode",
    )(page_table.reshape(-1), seq3(q), seq3(k_new), seq3(v_new), lf_new8,
      *([cache_k] * n_pages), *([cache_v] * n_pages), *([cache_lf8] * n_pages),
      rowmask, outmask, scale_rows, bias_pages, bias_self, g_row, dlam)
    return out.reshape(n_seq, d)


def _rel_bucket(dist):
    max_exact = N_BUCKETS // 2
    d = jnp.maximum(dist, 0)
    far = max_exact + (jnp.log(jnp.maximum(d, 1).astype(F32) / max_exact)
                       / math.log(MAX_DISTANCE / max_exact) * (N_BUCKETS - max_exact)).astype(jnp.int32)
    return jnp.where(d < max_exact, d, jnp.minimum(far, N_BUCKETS - 1))


def _prompt_bias_tiles(rel_bias_cols):
    assert ATT_BLOCK >= MAX_DISTANCE
    r = jnp.arange(ATT_BLOCK, dtype=jnp.int32)[:, None]
    c = jnp.arange(ATT_BLOCK, dtype=jnp.int32)[None, :]
    buckets = jnp.arange(N_BUCKETS, dtype=jnp.int32)[None, None, :]
    table = jnp.transpose(rel_bias_cols).astype(F32)[:, None, None, :]
    tiles = []
    for off in range(3):
        onehot = _rel_bucket(r - c + off * ATT_BLOCK)[:, :, None] == buckets
        tiles.append(jnp.sum(jnp.where(onehot[None], table, 0.0), axis=-1))
    return jnp.stack(tiles, axis=1)


def _row_heads():
    heads = [-1] * DEC_ROWS
    kind = ["pad"] * DEC_ROWS
    for i in range(HEADS_PER_MIXER):
        heads[ROW_SB + i], kind[ROW_SB + i] = 0 * HEADS_PER_MIXER + i, "full"
        heads[ROW_DIFF1 + i], kind[ROW_DIFF1 + i] = 1 * HEADS_PER_MIXER + i, "lo"
        heads[ROW_DIFF2 + i], kind[ROW_DIFF2 + i] = 1 * HEADS_PER_MIXER + i, "hi"
        heads[ROW_FOX + i], kind[ROW_FOX + i] = 2 * HEADS_PER_MIXER + i, "full"
        heads[ROW_MOBA + i], kind[ROW_MOBA + i] = 3 * HEADS_PER_MIXER + i, "full"
    return heads, kind


def _decode_consts(d):
    import numpy as np
    heads, kind = _row_heads()
    rowmask = np.zeros((DEC_ROWS, d), np.float32)
    outmask = np.zeros((DEC_ROWS, d), np.float32)
    scale = np.zeros((DEC_ROWS, LANES), np.float32)
    half = HEAD_DIM // 2
    for r in range(DEC_ROWS):
        if heads[r] < 0:
            continue
        lo = heads[r] * HEAD_DIM
        outmask[r, lo:lo + HEAD_DIM] = 1.0
        if kind[r] == "full":
            rowmask[r, lo:lo + HEAD_DIM] = 1.0
            scale[r] = HEAD_DIM ** -0.5
        elif kind[r] == "lo":
            rowmask[r, lo:lo + half] = 1.0
            scale[r] = half ** -0.5
        else:
            rowmask[r, lo + half:lo + HEAD_DIM] = 1.0
            scale[r] = half ** -0.5
    return jnp.asarray(rowmask), jnp.asarray(outmask), jnp.asarray(scale)


def _decode_bias(rel_bias, n_past, page):
    heads, _ = _row_heads()
    n_pages = n_past // page
    dist = n_past - jnp.arange(n_past, dtype=jnp.int32)
    per_key = rel_bias[_rel_bucket(dist)]
    self_b = rel_bias[_rel_bucket(jnp.zeros((), jnp.int32))]
    rows_k, rows_s = [], []
    for r in range(DEC_ROWS):
        h = heads[r]
        if h >= 3 * HEADS_PER_MIXER:
            c = HEADS_PER_MIXER + h - 3 * HEADS_PER_MIXER
        elif HEADS_PER_MIXER <= h < 2 * HEADS_PER_MIXER:
            c = h - HEADS_PER_MIXER
        else:
            c = None
        rows_k.append(jnp.zeros((n_past,), F32) if c is None else per_key[:, c])
        rows_s.append(jnp.zeros((), F32) if c is None else self_b[c])
    bias_keys = jnp.stack(rows_k).reshape(DEC_ROWS, n_pages, page).transpose(1, 0, 2)
    bias_self = jnp.broadcast_to(jnp.stack(rows_s)[:, None], (DEC_ROWS, LANES))
    return bias_keys.astype(F32), bias_self.astype(F32)


def _rows8(lf, n_heads):
    x = jnp.pad(lf[:, :n_heads], ((0, 0), (0, 8 - n_heads)))
    return jnp.broadcast_to(x[:, :, None], (x.shape[0], 8, LANES))


def kernel(x_prompt, x_sample, cache_k, cache_v, cache_logf, page_table, w_in, b_f, w_out,
           ffn_w_gu, ffn_w_down, ln_g, ln_b, rel_bias, diff_lambda, diff_subln_g):
    bsz, seq, d = x_prompt.shape
    n_dec, dec_seq, _ = x_sample.shape
    assert dec_seq == 1
    depth = w_in.shape[0]
    width = N_HEADS * HEAD_DIM
    n_pool, page = cache_k.shape[1], cache_k.shape[2]
    n_past = page_table.shape[1] * page
    alpha = (2 * depth) ** 0.25

    wgu = ffn_w_gu.astype(BF16)
    wdn = ffn_w_down.astype(BF16)
    w_qkv = w_in[:, :, :3 * width].astype(BF16)
    w_f = jnp.pad(w_in[:, :, 3 * width:], ((0, 0), (0, 0), (0, GATE_COLS - HEADS_PER_MIXER))).astype(BF16)
    b_f_row = jnp.pad(b_f, ((0, 0), (0, GATE_COLS - HEADS_PER_MIXER))).reshape(depth, 1, GATE_COLS)
    w_o = w_out.astype(BF16)
    ln_g3 = ln_g.reshape(depth * 3, 1, d)
    ln_b3 = ln_b.reshape(depth * 3, 1, d)

    bias_b = _prompt_bias_tiles(rel_bias[:, :HEADS_PER_MIXER])
    bias_d = _prompt_bias_tiles(rel_bias[:, HEADS_PER_MIXER:])
    dec_consts = _decode_consts(d)
    dec_bias, dec_bias_self = _decode_bias(rel_bias, n_past, page)
    cache_k4 = jnp.transpose(cache_k, (0, 1, 3, 4, 2)).reshape(depth, n_pool, width, page)
    cache_v4 = jnp.transpose(cache_v, (0, 1, 3, 4, 2)).reshape(depth, n_pool, width, page)
    cache_lf8 = jnp.pad(jnp.swapaxes(cache_logf, 2, 3), ((0, 0), (0, 0), (0, 8 - HEADS_PER_MIXER), (0, 0)))

    tm = 512
    xp = x_prompt.reshape(bsz * seq, d)
    xs = x_sample.reshape(n_dec, d)
    ks, vs, lfs, ksd, vsd, lfsd = [], [], [], [], [], []
    for lyr in range(depth):
        lam_init = 0.8 - 0.6 * math.exp(-0.3 * lyr)
        g_head = diff_subln_g[lyr].reshape(1, HEAD_DIM)
        g_row = jnp.tile(diff_subln_g[lyr], N_HEADS).reshape(1, d)
        dlam = diff_lambda[lyr]

        xp = _ffn_ln(xp, wgu, wdn, ln_g3, ln_b3, lyr, 0, alpha, tm)
        qb, q32, kf, vf, kb, vb, lf, cum, cumt = _inproj(xp, w_qkv, w_f, b_f_row, lyr, tm, seq)
        sh = lambda a: a.reshape(bsz, seq, a.shape[-1])
        o_parts = _prompt_attention(sh(qb), sh(q32), sh(kb), sh(vb), sh(kf), sh(cum), cumt,
                                    bias_b, bias_d, g_head, dlam, lam_init)
        xp = _outproj_ln(xp, o_parts, w_o, ln_g3, ln_b3, lyr, alpha, tm)
        xp = _ffn_ln(xp, wgu, wdn, ln_g3, ln_b3, lyr, 1, alpha, tm)
        ks.append(kf.reshape(bsz, seq, N_HEADS, HEAD_DIM))
        vs.append(vf.reshape(bsz, seq, N_HEADS, HEAD_DIM))
        lfs.append(lf[:, :HEADS_PER_MIXER].reshape(bsz, seq, HEADS_PER_MIXER))

        xs = _ffn_ln(xs, wgu, wdn, ln_g3, ln_b3, lyr, 0, alpha, n_dec)
        qd, kd, vd, lfd = _inproj_dec(xs, w_qkv, w_f, b_f_row, lyr)
        od = _decode_attention(qd, kd, vd, _rows8(lfd, HEADS_PER_MIXER), cache_k4, cache_v4, cache_lf8,
                               page_table, lyr, dec_consts, dec_bias, dec_bias_self, g_row, dlam,
                               lam_init)
        xs = _outproj_ln(xs, [od], w_o, ln_g3, ln_b3, lyr, alpha, n_dec)
        xs = _ffn_ln(xs, wgu, wdn, ln_g3, ln_b3, lyr, 1, alpha, n_dec)
        ksd.append(kd.reshape(n_dec, 1, N_HEADS, HEAD_DIM))
        vsd.append(vd.reshape(n_dec, 1, N_HEADS, HEAD_DIM))
        lfsd.append(lfd[:, :HEADS_PER_MIXER].reshape(n_dec, 1, HEADS_PER_MIXER))

    return (xp.reshape(bsz, seq, d), xs.reshape(n_dec, 1, d),
            jnp.stack(ks), jnp.stack(vs), jnp.stack(lfs),
            jnp.stack(ksd), jnp.stack(vsd), jnp.stack(lfsd))
```

```python
import functools
import math

import jax
import jax.numpy as jnp
from jax import lax
from jax.experimental import pallas as pl
from jax.experimental.pallas import tpu as pltpu

F32 = jnp.float32
BF16 = jnp.bfloat16

N_MIXERS = 4
HEADS_PER_MIXER = 4
N_HEADS = N_MIXERS * HEADS_PER_MIXER
HEAD_DIM = 64
MIX_COLS = HEADS_PER_MIXER * HEAD_DIM
N_BUCKETS = 32
MAX_DISTANCE = 128
MOBA_BLOCK = 256
MOBA_TOPK = 3
LN_EPS = 1e-5
RMS_EPS = 1e-5
NEG = -1e30

LANES = 128
GATE_COLS = LANES
VMEM_LIMIT = 56 * 2 ** 20
ATT_BLOCK = MOBA_BLOCK
DEC_ROWS = 32

NT_DIMS = (((1,), (1,)), ((), ()))


def _cparams(n_axes):
    return pltpu.CompilerParams(dimension_semantics=("arbitrary",) * n_axes,
                                vmem_limit_bytes=VMEM_LIMIT)


def _resident(block_shape, index_map):
    return pl.BlockSpec(block_shape, index_map, pipeline_mode=pl.Buffered(1))


def _layernorm(y, g, b):
    mu = jnp.mean(y, axis=-1, keepdims=True)
    d = y - mu
    var = jnp.mean(d * d, axis=-1, keepdims=True)
    return d * lax.rsqrt(var + LN_EPS) * g + b


def _log_sigmoid_pair(z):
    lg = jnp.log(1.0 + jnp.exp(-jnp.abs(z)))
    ls = jnp.minimum(z, 0.0) - lg
    return ls, ls - z


def _split_bf16(x, n):
    parts = []
    r = x
    for i in range(n):
        hi = r.astype(BF16)
        parts.append(hi)
        if i + 1 < n:
            r = r - hi.astype(F32)
    return parts


def _dot_exact_rhs(x, u, n):
    out = None
    for part in _split_bf16(x, n):
        d = jnp.dot(part, u, preferred_element_type=F32)
        out = d if out is None else out + d
    return out


def _dot_f32(a, b, dims):
    a1, a2, a3 = _split_bf16(a, 3)
    b1, b2, b3 = _split_bf16(b, 3)

    def d(x, y):
        return lax.dot_general(x, y, dims, preferred_element_type=F32)

    return d(a1, b1) + (d(a1, b2) + d(a2, b1)) + (d(a1, b3) + d(a2, b2) + d(a3, b1))


def _topk_select(gates, n_valid, n_max, block, axis=1):
    valid = block < n_valid
    gv = jnp.where(valid, gates, NEG)
    rank = jnp.zeros(gates.shape, F32)
    for m in range(n_max):
        cand = gv[:, m:m + 1] if axis == 1 else gv[m:m + 1, :]
        beats = (cand > gv) | ((cand == gv) & (m < block))
        rank = rank + beats.astype(F32)
    return jnp.where((rank < MOBA_TOPK) & valid, 1.0, 0.0)


def _ffn_kernel(x_ref, wgu_ref, wd_ref, g_ref, b_ref, o_ref, *, d_ff, chunk, alpha):
    x = x_ref[...]
    xb = x.astype(BF16)
    acc = jnp.zeros(x.shape, F32)
    for c in range(d_ff // chunk):
        gate = jnp.dot(xb, wgu_ref[:, c * chunk:(c + 1) * chunk], preferred_element_type=F32)
        up = jnp.dot(xb, wgu_ref[:, d_ff + c * chunk:d_ff + (c + 1) * chunk],
                     preferred_element_type=F32)
        h = (gate * jax.nn.sigmoid(gate) * up).astype(BF16)
        acc = acc + jnp.dot(h, wd_ref[c * chunk:(c + 1) * chunk, :], preferred_element_type=F32)
    o_ref[...] = _layernorm(alpha * x + 0.5 * acc, g_ref[...], b_ref[...])


def _ffn_ln(x, wgu, wd, ln_g, ln_b, lyr, idx, alpha, tm):
    n, d = x.shape
    d_ff = wd.shape[2]
    chunk = 256
    assert n % tm == 0 and d_ff % chunk == 0
    return pl.pallas_call(
        functools.partial(_ffn_kernel, d_ff=d_ff, chunk=chunk, alpha=alpha),
        grid=(n // tm,),
        in_specs=[
            pl.BlockSpec((tm, d), lambda i: (i, 0)),
            _resident((None, None, d, 2 * d_ff), lambda i: (lyr, idx, 0, 0)),
            _resident((None, None, d_ff, d), lambda i: (lyr, idx, 0, 0)),
            _resident((None, 1, d), lambda i: (lyr * 3 + 2 * idx, 0, 0)),
            _resident((None, 1, d), lambda i: (lyr * 3 + 2 * idx, 0, 0)),
        ],
        out_specs=pl.BlockSpec((tm, d), lambda i: (i, 0)),
        out_shape=jax.ShapeDtypeStruct((n, d), F32),
        compiler_params=_cparams(1),
        name="ffn_ln",
    )(x, wgu, wd, ln_g, ln_b)


def _inproj_kernel(x_ref, w_ref, wf_ref, bf_ref, qb_ref, q32_ref, kf_ref, vf_ref, kb_ref, vb_ref,
                   lf_ref, cum_ref, cumt_ref, carry_ref, *, width, tiles_per_seq, q_cols):
    i = pl.program_id(0)
    xb = x_ref[...].astype(BF16)
    q = jnp.dot(xb, w_ref[:, 0:width], preferred_element_type=F32)
    qb_ref[...] = q.astype(BF16)
    q32_ref[...] = q[:, width - q_cols:]
    k = jnp.dot(xb, w_ref[:, width:2 * width], preferred_element_type=F32)
    kf_ref[...] = k
    kb_ref[...] = k.astype(BF16)
    v = jnp.dot(xb, w_ref[:, 2 * width:3 * width], preferred_element_type=F32)
    vf_ref[...] = v
    vb_ref[...] = v.astype(BF16)
    z = jnp.dot(xb, wf_ref[...], preferred_element_type=F32) + bf_ref[...]
    lf, _ = _log_sigmoid_pair(z)
    lf_ref[...] = lf

    @pl.when(i % tiles_per_seq == 0)
    def _():
        carry_ref[...] = jnp.zeros_like(carry_ref)

    tm = lf.shape[0]
    row = lax.broadcasted_iota(jnp.int32, (tm, tm), 0)
    col = lax.broadcasted_iota(jnp.int32, (tm, tm), 1)
    lower = (col <= row).astype(BF16)
    cum = carry_ref[0:1, :]
    for part in _split_bf16(lf, 3):
        cum = cum + jnp.dot(lower, part, preferred_element_type=F32)
    carry_ref[...] = jnp.broadcast_to(cum[tm - 1:tm, :], carry_ref.shape)
    cum_ref[...] = cum
    cumt_ref[0] = cum.T[0:8, :]


def _inproj(x, w_qkv, w_f, b_f, lyr, tm, seq_len):
    n, d = x.shape
    width = w_qkv.shape[2] // 3
    assert n % tm == 0 and seq_len % tm == 0
    tiles_per_seq = seq_len // tm
    n_seq = n // seq_len
    row_spec = lambda cols: pl.BlockSpec((tm, cols), lambda i: (i, 0))
    outs = pl.pallas_call(
        functools.partial(_inproj_kernel, width=width, tiles_per_seq=tiles_per_seq, q_cols=MIX_COLS),
        grid=(n // tm,),
        in_specs=[
            pl.BlockSpec((tm, d), lambda i: (i, 0)),
            _resident((None, d, 3 * width), lambda i: (lyr, 0, 0)),
            _resident((None, d, GATE_COLS), lambda i: (lyr, 0, 0)),
            _resident((None, 1, GATE_COLS), lambda i: (lyr, 0, 0)),
        ],
        out_specs=[
            row_spec(width), row_spec(MIX_COLS), row_spec(width), row_spec(width),
            row_spec(width), row_spec(width), row_spec(GATE_COLS), row_spec(GATE_COLS),
            pl.BlockSpec((1, 8, tm), lambda i: (i // tiles_per_seq, 0, i % tiles_per_seq)),
        ],
        out_shape=[
            jax.ShapeDtypeStruct((n, width), BF16),
            jax.ShapeDtypeStruct((n, MIX_COLS), F32),
            jax.ShapeDtypeStruct((n, width), F32),
            jax.ShapeDtypeStruct((n, width), F32),
            jax.ShapeDtypeStruct((n, width), BF16),
            jax.ShapeDtypeStruct((n, width), BF16),
            jax.ShapeDtypeStruct((n, GATE_COLS), F32),
            jax.ShapeDtypeStruct((n, GATE_COLS), F32),
            jax.ShapeDtypeStruct((n_seq, 8, seq_len), F32),
        ],
        scratch_shapes=[pltpu.VMEM((8, GATE_COLS), F32)],
        compiler_params=_cparams(1),
        name="inproj",
    )(x, w_qkv, w_f, b_f)
    return outs


def _inproj_dec_kernel(x_ref, w_ref, wf_ref, bf_ref, q_ref, k_ref, v_ref, lf_ref, *, width):
    xb = x_ref[...].astype(BF16)
    q_ref[...] = jnp.dot(xb, w_ref[:, 0:width], preferred_element_type=F32)
    k_ref[...] = jnp.dot(xb, w_ref[:, width:2 * width], preferred_element_type=F32)
    v_ref[...] = jnp.dot(xb, w_ref[:, 2 * width:3 * width], preferred_element_type=F32)
    z = jnp.dot(xb, wf_ref[...], preferred_element_type=F32) + bf_ref[...]
    lf_ref[...] = _log_sigmoid_pair(z)[0]


def _inproj_dec(x, w_qkv, w_f, b_f, lyr):
    n, d = x.shape
    width = w_qkv.shape[2] // 3
    full = lambda cols: pl.BlockSpec((n, cols), lambda i: (0, 0))
    return pl.pallas_call(
        functools.partial(_inproj_dec_kernel, width=width),
        grid=(1,),
        in_specs=[
            full(d),
            pl.BlockSpec((None, d, 3 * width), lambda i: (lyr, 0, 0)),
            pl.BlockSpec((None, d, GATE_COLS), lambda i: (lyr, 0, 0)),
            pl.BlockSpec((None, 1, GATE_COLS), lambda i: (lyr, 0, 0)),
        ],
        out_specs=[full(width), full(width), full(width), full(GATE_COLS)],
        out_shape=[jax.ShapeDtypeStruct((n, width), F32)] * 3
        + [jax.ShapeDtypeStruct((n, GATE_COLS), F32)],
        compiler_params=_cparams(1),
        name="inproj_dec",
    )(x, w_qkv, w_f, b_f)


def _outproj_kernel(x_ref, *refs, alpha, n_parts):
    o_refs = refs[:n_parts]
    w_ref, g_ref, b_ref, y_ref = refs[n_parts:]
    acc = None
    c0 = 0
    for o_ref in o_refs:
        cols = o_ref.shape[1]
        d = jnp.dot(o_ref[...], w_ref[c0:c0 + cols, :], preferred_element_type=F32)
        acc = d if acc is None else acc + d
        c0 += cols
    y_ref[...] = _layernorm(alpha * x_ref[...] + acc, g_ref[...], b_ref[...])


def _outproj_ln(x, o_parts, w_out, ln_g, ln_b, lyr, alpha, tm):
    n, d = x.shape
    assert n % tm == 0
    return pl.pallas_call(
        functools.partial(_outproj_kernel, alpha=alpha, n_parts=len(o_parts)),
        grid=(n // tm,),
        in_specs=[pl.BlockSpec((tm, d), lambda i: (i, 0))]
        + [pl.BlockSpec((tm, o.shape[1]), lambda i: (i, 0)) for o in o_parts]
        + [
            _resident((None, w_out.shape[1], d), lambda i: (lyr, 0, 0)),
            _resident((None, 1, d), lambda i: (lyr * 3 + 1, 0, 0)),
            _resident((None, 1, d), lambda i: (lyr * 3 + 1, 0, 0)),
        ],
        out_specs=pl.BlockSpec((tm, d), lambda i: (i, 0)),
        out_shape=jax.ShapeDtypeStruct((n, d), F32),
        compiler_params=_cparams(1),
        name="outproj_ln",
    )(x, *o_parts, w_out, ln_g, ln_b)


_HEADS = range(HEADS_PER_MIXER)


def _head_cols(h):
    return slice(h * HEAD_DIM, (h + 1) * HEAD_DIM)


def _kv_block(ref, j, cols):
    start = pl.multiple_of(j * ATT_BLOCK, ATT_BLOCK)
    return ref[0, pl.ds(start, ATT_BLOCK), cols]


def _softmax_step(s, vb, state, keep=None):
    m, l, acc = state
    if keep is not None:
        s = jnp.where(keep, s, NEG)
    m_new = jnp.maximum(m, jnp.max(s, axis=-1, keepdims=True))
    alpha = jnp.exp(m - m_new)
    p = jnp.exp(s - m_new)
    l = alpha * l + jnp.sum(p, axis=-1, keepdims=True)
    acc = alpha * acc + jnp.dot(p.astype(BF16), vb, preferred_element_type=F32)
    return m_new, l, acc


def _softmax_init(rows, cols):
    return (jnp.full((rows, 1), NEG, F32), jnp.zeros((rows, 1), F32), jnp.zeros((rows, cols), F32))


def _block_iotas():
    r = lax.broadcasted_iota(jnp.int32, (ATT_BLOCK, ATT_BLOCK), 0)
    c = lax.broadcasted_iota(jnp.int32, (ATT_BLOCK, ATT_BLOCK), 1)
    return r, c


ROW_TILE = ATT_BLOCK
MOBA_ROW_TILE = ATT_BLOCK // 2
_TILES = range(ATT_BLOCK // ROW_TILE)


def _tile_rows(t, rt=ROW_TILE):
    return slice(t * rt, (t + 1) * rt)


def _tile_iotas(t, rt=ROW_TILE):
    r = lax.broadcasted_iota(jnp.int32, (rt, ATT_BLOCK), 0) + t * rt
    c = lax.broadcasted_iota(jnp.int32, (rt, ATT_BLOCK), 1)
    return r, c


def _load_q_tiles(q_ref, scale=None, rt=ROW_TILE):
    def load(h, t):
        q = q_ref[0, _tile_rows(t, rt), _head_cols(h)]
        return q if scale is None else q * scale
    return [[load(h, t) for t in range(ATT_BLOCK // rt)] for h in _HEADS]


def _store_head(o_ref, h, tiles):
    o_ref[0, :, _head_cols(h)] = jnp.concatenate(tiles, axis=0).astype(o_ref.dtype)


def _walk_blocks(qi, init, load, phases, diagonal_first=False, descending=False, tiles=_TILES):
    def step(j, carry, diagonal):
        blocks = [load(h, j) for h in _HEADS]
        vals = [[carry[h][t] for t in tiles] for h in _HEADS]
        for phase in phases:
            vals = [[phase(h, t, blocks[h], vals[h][t], diagonal) for t in tiles] for h in _HEADS]
        return tuple(tuple(v) for v in vals)

    carry = tuple(tuple(init() for _ in tiles) for _ in _HEADS)
    if diagonal_first:
        carry = step(qi, carry, True)
    if descending:
        carry = lax.fori_loop(0, qi, lambda jj, cr: step(qi - 1 - jj, cr, False), carry)
    else:
        carry = lax.fori_loop(0, qi, lambda j, cr: step(j, cr, False), carry)
    if not diagonal_first:
        carry = step(qi, carry, True)
    return carry


def _sb_kernel(q_ref, k_ref, v_ref, o_ref, *, scale):
    qi = pl.program_id(1)
    r, c = _block_iotas()
    suffix = (r > c).astype(BF16)

    qs = _load_q_tiles(q_ref, scale)

    def load(h, j):
        return _kv_block(k_ref, j, _head_cols(h)), _kv_block(v_ref, j, _head_cols(h))

    def scores(h, t, blocks, state, diagonal):
        return state, lax.dot_general(qs[h][t], blocks[0], NT_DIMS, preferred_element_type=F32)

    def log_terms(h, t, blocks, value, diagonal):
        state, z = value
        ls, l1m = _log_sigmoid_pair(z)
        if diagonal:
            rt, ct = _tile_iotas(t)
            l1m = jnp.where(ct < rt, l1m, 0.0)
        return state, ls, l1m[:, 0:1], _dot_exact_rhs(l1m, suffix, 2)

    def weigh(h, t, blocks, value, diagonal):
        (later, acc), ls, l1m_first, rest_local = value
        w = jnp.exp(ls + (rest_local + later))
        if diagonal:
            rt, ct = _tile_iotas(t)
            w = jnp.where(ct < rt, w, 0.0)
        acc = acc + jnp.dot(w.astype(BF16), blocks[1], preferred_element_type=F32)
        return later + (rest_local[:, 0:1] + l1m_first), acc

    def init():
        return jnp.zeros((ROW_TILE, 1), F32), jnp.zeros((ROW_TILE, HEAD_DIM), F32)

    carry = _walk_blocks(qi, init, load, [scores, log_terms, weigh], diagonal_first=True,
                         descending=True)
    for h in _HEADS:
        _store_head(o_ref, h, [carry[h][t][1] for t in _TILES])


def _diff_kernel(q_ref, k_ref, v_ref, bias_ref, g_ref, dl_ref, o_ref, *, scale, lam_init):
    qi = pl.program_id(1)
    half = HEAD_DIM // 2
    dl = dl_ref[...]
    lam = (jnp.exp(jnp.sum(dl[0:1] * dl[1:2], axis=-1, keepdims=True))
           - jnp.exp(jnp.sum(dl[2:3] * dl[3:4], axis=-1, keepdims=True)) + lam_init)

    def half_cols(h, i):
        return slice(h * HEAD_DIM + i * half, h * HEAD_DIM + (i + 1) * half)

    qs = [[[q_ref[0, _tile_rows(t), half_cols(h, i)] for i in range(2)] for t in _TILES]
          for h in _HEADS]

    def load(h, j):
        return ([_kv_block(k_ref, j, half_cols(h, i)) for i in range(2)],
                _kv_block(v_ref, j, _head_cols(h)), jnp.minimum(qi - j, 2))

    def scores(h, t, blocks, state, diagonal):
        return state, [lax.dot_general(qs[h][t][i], blocks[0][i], NT_DIMS, preferred_element_type=F32)
                       for i in range(2)]

    def attend(h, t, blocks, value, diagonal):
        state, raw = value
        _, vb, kind = blocks
        bias = bias_ref[h, kind, _tile_rows(t), :]
        keep = None
        if diagonal:
            rt, ct = _tile_iotas(t)
            keep = ct <= rt
        return tuple(_softmax_step(raw[i] * scale + bias, vb, state[i], keep) for i in range(2))

    def init():
        return _softmax_init(ROW_TILE, HEAD_DIM), _softmax_init(ROW_TILE, HEAD_DIM)

    carry = _walk_blocks(qi, init, load, [scores, attend])
    for h in _HEADS:
        tiles = []
        for t in _TILES:
            (_, l1, a1), (_, l2, a2) = carry[h][t]
            o = a1 / l1 - lam * (a2 / l2)
            o = o * lax.rsqrt(jnp.mean(o * o, axis=-1, keepdims=True) + RMS_EPS)
            tiles.append(o * g_ref[...] * (1.0 - lam_init))
        _store_head(o_ref, h, tiles)


def _fox_kernel(q_ref, k_ref, v_ref, cum_ref, cumt_ref, o_ref, *, scale):
    qi = pl.program_id(1)
    qs = _load_q_tiles(q_ref, scale)
    fqs = [[jnp.broadcast_to(cum_ref[0, _tile_rows(t), h:h + 1], (ROW_TILE, ATT_BLOCK))
            for t in _TILES] for h in _HEADS]

    def load(h, j):
        start = pl.multiple_of(j * ATT_BLOCK, ATT_BLOCK)
        return (_kv_block(k_ref, j, _head_cols(h)), _kv_block(v_ref, j, _head_cols(h)),
                cumt_ref[0, h:h + 1, pl.ds(start, ATT_BLOCK)])

    def scores(h, t, blocks, state, diagonal):
        return state, lax.dot_general(qs[h][t], blocks[0], NT_DIMS, preferred_element_type=F32)

    def attend(h, t, blocks, value, diagonal):
        state, raw = value
        _, vb, fk = blocks
        keep = None
        if diagonal:
            rt, ct = _tile_iotas(t)
            keep = ct <= rt
        return _softmax_step(raw + (fqs[h][t] - fk), vb, state, keep)

    carry = _walk_blocks(qi, lambda: _softmax_init(ROW_TILE, HEAD_DIM), load, [scores, attend])
    for h in _HEADS:
        _store_head(o_ref, h, [carry[h][t][2] / carry[h][t][1] for t in _TILES])


def _moba_kernel(q_ref, q32_ref, k_ref, v_ref, k32_ref, bias_ref, o_ref, kmean_ref, *, scale, n_blocks):
    qi = pl.program_id(1)
    rt = MOBA_ROW_TILE
    tiles = range(ATT_BLOCK // rt)
    lane = lax.broadcasted_iota(jnp.int32, (rt, LANES), 1)

    @pl.when(qi == 0)
    def _():
        kmean_ref[...] = jnp.zeros_like(kmean_ref)
        for h in range(HEADS_PER_MIXER):
            for n in range(n_blocks):
                blk = k32_ref[0, n * MOBA_BLOCK:(n + 1) * MOBA_BLOCK, _head_cols(h)]
                kmean_ref[h, n:n + 1, :] = jnp.mean(blk, axis=0, keepdims=True)

    qs = _load_q_tiles(q_ref, scale, rt)
    nb8 = -(-n_blocks // 8) * 8
    assert nb8 <= LANES
    blk = lax.broadcasted_iota(jnp.int32, (nb8, ATT_BLOCK), 0)

    def select(h):
        gates_t = _dot_f32(kmean_ref[h, 0:nb8, :], q32_ref[0, :, _head_cols(h)], NT_DIMS)
        sel_t = _topk_select(gates_t, qi, n_blocks, blk, axis=0)
        sel = jnp.concatenate([sel_t, jnp.zeros((LANES - nb8, ATT_BLOCK), F32)], axis=0).T
        return [sel[_tile_rows(t, rt)] for t in tiles]

    sels = [select(h) for h in _HEADS]

    def load(h, j):
        return (_kv_block(k_ref, j, _head_cols(h)), _kv_block(v_ref, j, _head_cols(h)),
                jnp.minimum(qi - j, 2), j)

    def scores(h, t, blocks, state, diagonal):
        return state, lax.dot_general(qs[h][t], blocks[0], NT_DIMS, preferred_element_type=F32)

    def attend(h, t, blocks, value, diagonal):
        state, raw = value
        _, vb, kind, j = blocks
        if diagonal:
            rows, keys = _tile_iotas(t, rt)
            keep = keys <= rows
        else:
            keep = jnp.sum(jnp.where(lane == j, sels[h][t], 0.0), axis=-1, keepdims=True) > 0.0
        return _softmax_step(raw + bias_ref[h, kind, _tile_rows(t, rt), :], vb, state, keep)

    carry = _walk_blocks(qi, lambda: _softmax_init(rt, HEAD_DIM), load, [scores, attend],
                         diagonal_first=True, tiles=tiles)
    for h in _HEADS:
        _store_head(o_ref, h, [carry[h][t][2] / carry[h][t][1] for t in tiles])


def _prompt_attention(qb, q32, kb, vb, k32, cum, cumt, bias_b, bias_d, subln_g, dlam, lam_init):
    bsz, seq, _ = qb.shape
    assert seq % ATT_BLOCK == 0
    nq = seq // ATT_BLOCK
    grid = (bsz, nq)

    def q_spec(m, cols=MIX_COLS):
        return pl.BlockSpec((1, ATT_BLOCK, cols), lambda b, i: (b, i, m))

    def kv_spec(m):
        return pl.BlockSpec((1, seq, MIX_COLS), lambda b, i: (b, 0, m))

    bias_spec = _resident((HEADS_PER_MIXER, 3, ATT_BLOCK, ATT_BLOCK), lambda b, i: (0, 0, 0, 0))
    out_spec = pl.BlockSpec((1, ATT_BLOCK, MIX_COLS), lambda b, i: (b, i, 0))
    out_shape = jax.ShapeDtypeStruct((bsz, seq, MIX_COLS), BF16)
    scale = HEAD_DIM ** -0.5

    o_a = pl.pallas_call(
        functools.partial(_sb_kernel, scale=scale),
        grid=grid, in_specs=[q_spec(0), kv_spec(0), kv_spec(0)],
        out_specs=out_spec, out_shape=out_shape, compiler_params=_cparams(2), name="attn_sb",
    )(qb, kb, vb)

    o_b = pl.pallas_call(
        functools.partial(_diff_kernel, scale=(HEAD_DIM // 2) ** -0.5, lam_init=lam_init),
        grid=grid,
        in_specs=[q_spec(1), kv_spec(1), kv_spec(1), bias_spec,
                  _resident((1, HEAD_DIM), lambda b, i: (0, 0)),
                  _resident(dlam.shape, lambda b, i: (0, 0))],
        out_specs=out_spec, out_shape=out_shape, compiler_params=_cparams(2), name="attn_diff",
    )(qb, kb, vb, bias_b, subln_g, dlam)

    o_c = pl.pallas_call(
        functools.partial(_fox_kernel, scale=scale),
        grid=grid,
        in_specs=[q_spec(2), kv_spec(2), kv_spec(2),
                  pl.BlockSpec((1, ATT_BLOCK, GATE_COLS), lambda b, i: (b, i, 0)),
                  pl.BlockSpec((1, 8, seq), lambda b, i: (b, 0, 0))],
        out_specs=out_spec, out_shape=out_shape, compiler_params=_cparams(2), name="attn_fox",
    )(qb, kb, vb, cum, cumt)

    o_d = pl.pallas_call(
        functools.partial(_moba_kernel, scale=scale, n_blocks=seq // MOBA_BLOCK),
        grid=grid,
        in_specs=[q_spec(3), q_spec(0), kv_spec(3), kv_spec(3), kv_spec(3), bias_spec],
        out_specs=out_spec, out_shape=out_shape,
        scratch_shapes=[pltpu.VMEM((HEADS_PER_MIXER, LANES, HEAD_DIM), F32)],
        compiler_params=_cparams(2), name="attn_moba",
    )(qb, q32, kb, vb, k32, bias_d)

    return [o.reshape(bsz * seq, MIX_COLS) for o in (o_a, o_b, o_c, o_d)]


ROW_FOX, ROW_SB, ROW_DIFF1, ROW_DIFF2, ROW_MOBA = 0, 4, 8, 12, 16


def _dec_seq_kernel(pt_ref, q_ref, kn_ref, vn_ref, lfn_ref, *refs, n_pages, pages_per_block, lam_init):
    del pt_ref
    k_refs = refs[:n_pages]
    v_refs = refs[n_pages:2 * n_pages]
    lf_refs = refs[2 * n_pages:3 * n_pages]
    rowmask_ref, outmask_ref, scale_ref, bias_ref, bself_ref, g_ref, dl_ref, o_ref = refs[3 * n_pages:]
    rows = DEC_ROWS
    page = k_refs[0].shape[1]
    group = 2 if n_pages % 2 == 0 else 1
    rowi = lax.broadcasted_iota(jnp.int32, (rows, LANES), 0)
    lane = lax.broadcasted_iota(jnp.int32, (rows, LANES), 1)
    row8 = rowi[0:8]
    is_sb = (rowi >= ROW_SB) & (rowi < ROW_DIFF1)
    is_moba = (rowi >= ROW_MOBA) & (rowi < ROW_MOBA + HEADS_PER_MIXER)
    is_diff2 = (rowi >= ROW_DIFF2) & (rowi < ROW_MOBA)

    q32 = q_ref[0] * rowmask_ref[...]
    qb = q32.astype(BF16)
    scale = scale_ref[...]

    s_pages = []
    for g0 in range(0, n_pages, group):
        kb = jnp.concatenate([k_refs[g][...].astype(BF16) for g in range(g0, g0 + group)], axis=1)
        sg = jnp.dot(qb, kb, preferred_element_type=F32)
        s_pages += [sg[:, i * page:(i + 1) * page] for i in range(group)]
    s_pages = [s_pages[p] * scale + bias_ref[p] for p in range(n_pages)]

    ls8, x8 = [], []
    for p in range(n_pages):
        ls, l1m = _log_sigmoid_pair(s_pages[p][0:8])
        ls8.append(ls)
        x8.append(jnp.where(row8 < ROW_SB, lf_refs[p][...], l1m))
    x_all = jnp.concatenate(x8, axis=0)
    jr = lax.broadcasted_iota(jnp.int32, (page, page), 0)
    jc = lax.broadcasted_iota(jnp.int32, (page, page), 1)
    y_all = _dot_exact_rhs(x_all, (jr > jc).astype(BF16), 3)
    t_all = jnp.sum(x_all, axis=-1, keepdims=True)

    run = jnp.where(row8 < ROW_SB, lfn_ref[0], 0.0)
    adj8 = [None] * n_pages
    for p in reversed(range(n_pages)):
        base = jnp.where(row8 < ROW_SB, s_pages[p][0:8], ls8[p])
        adj8[p] = base + (y_all[8 * p:8 * p + 8] + run)
        run = run + t_all[8 * p:8 * p + 8]

    n_blocks = n_pages // pages_per_block
    assert n_blocks <= LANES
    moba_rows = slice((N_MIXERS - 1) * MIX_COLS, N_MIXERS * MIX_COLS)
    lane_m = lax.broadcasted_iota(jnp.int32, (MIX_COLS, LANES), 1)
    ksum = jnp.zeros((MIX_COLS, LANES), F32)
    for n in range(n_blocks):
        col = None
        for p in range(n * pages_per_block, (n + 1) * pages_per_block):
            cs = jnp.sum(k_refs[p][moba_rows, :], axis=-1, keepdims=True)
            col = cs if col is None else col + cs
        ksum = jnp.where(lane_m == n, col, ksum)
    gates = _dot_f32(q32[:, moba_rows], ksum * (1.0 / MOBA_BLOCK), (((1,), (0,)), ((), ())))
    sel = _topk_select(gates, n_blocks, n_blocks, lane)

    s_eff = []
    for p in range(n_pages):
        full = jnp.concatenate([adj8[p], s_pages[p][8:]], axis=0)
        n = p // pages_per_block
        dropped = is_moba & jnp.logical_not(sel[:, n:n + 1] > 0.0)
        s_eff.append(jnp.where(dropped, NEG, full))
    s_self = (jnp.sum(q32 * kn_ref[0], axis=-1, keepdims=True) * scale + bself_ref[...])
    mx = s_eff[0]
    for p in range(1, n_pages):
        mx = jnp.maximum(mx, s_eff[p])
    big = jnp.maximum(jnp.max(mx, axis=-1, keepdims=True), s_self)
    probs = [jnp.exp(jnp.where(is_sb, s_eff[p], s_eff[p] - big)) for p in range(n_pages)]
    w_self = jnp.where(is_sb, 0.0, jnp.exp(s_self - big))
    psum = probs[0]
    for p in range(1, n_pages):
        psum = psum + probs[p]
    tot = jnp.sum(psum, axis=-1, keepdims=True) + w_self

    acc = w_self[:, 0:1] * vn_ref[0]
    for g0 in range(0, n_pages, group):
        pg = jnp.concatenate([probs[g] for g in range(g0, g0 + group)], axis=1).astype(BF16)
        vb = jnp.concatenate([v_refs[g][...].astype(BF16) for g in range(g0, g0 + group)], axis=1)
        acc = acc + lax.dot_general(pg, vb, NT_DIMS, preferred_element_type=F32)
    acc = acc / jnp.where(is_sb, 1.0, tot)[:, 0:1]

    dl = dl_ref[...]
    lam = (jnp.exp(jnp.sum(dl[0:1] * dl[1:2], axis=-1, keepdims=True))
           - jnp.exp(jnp.sum(dl[2:3] * dl[3:4], axis=-1, keepdims=True)) + lam_init)
    coef = jnp.where(is_diff2, -lam, 1.0)[:, 0:1]
    out = jnp.sum(acc * outmask_ref[...] * coef, axis=0, keepdims=True)

    col = lax.broadcasted_iota(jnp.int32, out.shape, 1)
    sq = out * out
    factor = jnp.ones(out.shape, F32)
    for h in _HEADS:
        lo = (HEADS_PER_MIXER + h) * HEAD_DIM
        in_head = (col >= lo) & (col < lo + HEAD_DIM)
        ms = jnp.sum(jnp.where(in_head, sq, 0.0), axis=-1, keepdims=True) * (1.0 / HEAD_DIM)
        factor = jnp.where(in_head, lax.rsqrt(ms + RMS_EPS) * (1.0 - lam_init), factor)
    is_diff_col = (col >= MIX_COLS) & (col < 2 * MIX_COLS)
    out = jnp.where(is_diff_col, out * factor * g_ref[...], out)
    o_ref[0] = out.astype(o_ref.dtype)


def _decode_attention(q, k_new, v_new, lf_new8, cache_k, cache_v, cache_lf8, page_table, lyr,
                      consts, bias_pages, bias_self, g_row, dlam, lam_init):
    n_seq, d = q.shape
    n_pages = page_table.shape[1]
    page = cache_k.shape[3]
    assert page == LANES and MOBA_BLOCK % page == 0 and (n_pages * page) % MOBA_BLOCK == 0
    rowmask, outmask, scale_rows = consts
    rows = DEC_ROWS
    seq3 = lambda a: a.reshape(n_seq, 1, a.shape[-1])

    def per_seq(cols):
        return pl.BlockSpec((1, 1, cols), lambda b, pt: (b, 0, 0))

    def const(shape):
        return pl.BlockSpec(shape, lambda b, pt: (0,) * len(shape))

    def paged(r, cols, p):
        return pl.BlockSpec((None, None, r, cols), lambda b, pt: (lyr, pt[b * n_pages + p], 0, 0))

    grid_spec = pltpu.PrefetchScalarGridSpec(
        num_scalar_prefetch=1,
        grid=(n_seq,),
        in_specs=[per_seq(d), per_seq(d), per_seq(d),
                  pl.BlockSpec((1, 8, LANES), lambda b, pt: (b, 0, 0))]
        + [paged(d, page, p) for p in range(n_pages)]
        + [paged(d, page, p) for p in range(n_pages)]
        + [paged(8, page, p) for p in range(n_pages)]
        + [const((rows, d)), const((rows, d)), const((rows, LANES)), const((n_pages, rows, page)),
           const((rows, LANES)), const((1, d)), const(dlam.shape)],
        out_specs=per_seq(d),
    )
    out = pl.pallas_call(
        functools.partial(_dec_seq_kernel, n_pages=n_pages, pages_per_block=MOBA_BLOCK // page,
                          lam_init=lam_init),
        grid_spec=grid_spec,
        out_shape=jax.ShapeDtypeStruct((n_seq, 1, d), BF16),
        compiler_params=_cparams(1),
        name="attn_decode",
    )(page_table.reshape(-1), seq3(q), seq3(k_new), seq3(v_new), lf_new8,
      *([cache_k] * n_pages), *([cache_v] * n_pages), *([cache_lf8] * n_pages),
      rowmask, outmask, scale_rows, bias_pages, bias_self, g_row, dlam)
    return out.reshape(n_seq, d)


def _rel_bucket(dist):
    max_exact = N_BUCKETS // 2
    d = jnp.maximum(dist, 0)
    far = max_exact + (jnp.log(jnp.maximum(d, 1).astype(F32) / max_exact)
                       / math.log(MAX_DISTANCE / max_exact) * (N_BUCKETS - max_exact)).astype(jnp.int32)
    return jnp.where(d < max_exact, d, jnp.minimum(far, N_BUCKETS - 1))


def _prompt_bias_tiles(rel_bias_cols):
    assert ATT_BLOCK >= MAX_DISTANCE
    r = jnp.arange(ATT_BLOCK, dtype=jnp.int32)[:, None]
    c = jnp.arange(ATT_BLOCK, dtype=jnp.int32)[None, :]
    buckets = jnp.arange(N_BUCKETS, dtype=jnp.int32)[None, None, :]
    table = jnp.transpose(rel_bias_cols).astype(F32)[:, None, None, :]
    tiles = []
    for off in range(3):
        onehot = _rel_bucket(r - c + off * ATT_BLOCK)[:, :, None] == buckets
        tiles.append(jnp.sum(jnp.where(onehot[None], table, 0.0), axis=-1))
    return jnp.stack(tiles, axis=1)


def _row_heads():
    heads = [-1] * DEC_ROWS
    kind = ["pad"] * DEC_ROWS
    for i in range(HEADS_PER_MIXER):
        heads[ROW_SB + i], kind[ROW_SB + i] = 0 * HEADS_PER_MIXER + i, "full"
        heads[ROW_DIFF1 + i], kind[ROW_DIFF1 + i] = 1 * HEADS_PER_MIXER + i, "lo"
        heads[ROW_DIFF2 + i], kind[ROW_DIFF2 + i] = 1 * HEADS_PER_MIXER + i, "hi"
        heads[ROW_FOX + i], kind[ROW_FOX + i] = 2 * HEADS_PER_MIXER + i, "full"
        heads[ROW_MOBA + i], kind[ROW_MOBA + i] = 3 * HEADS_PER_MIXER + i, "full"
    return heads, kind


def _decode_consts(d):
    import numpy as np
    heads, kind = _row_heads()
    rowmask = np.zeros((DEC_ROWS, d), np.float32)
    outmask = np.zeros((DEC_ROWS, d), np.float32)
    scale = np.zeros((DEC_ROWS, LANES), np.float32)
    half = HEAD_DIM // 2
    for r in range(DEC_ROWS):
        if heads[r] < 0:
            continue
        lo = heads[r] * HEAD_DIM
        outmask[r, lo:lo + HEAD_DIM] = 1.0
        if kind[r] == "full":
            rowmask[r, lo:lo + HEAD_DIM] = 1.0
            scale[r] = HEAD_DIM ** -0.5
        elif kind[r] == "lo":
            rowmask[r, lo:lo + half] = 1.0
            scale[r] = half ** -0.5
        else:
            rowmask[r, lo + half:lo + HEAD_DIM] = 1.0
            scale[r] = half ** -0.5
    return jnp.asarray(rowmask), jnp.asarray(outmask), jnp.asarray(scale)


def _decode_bias(rel_bias, n_past, page):
    heads, _ = _row_heads()
    n_pages = n_past // page
    dist = n_past - jnp.arange(n_past, dtype=jnp.int32)
    per_key = rel_bias[_rel_bucket(dist)]
    self_b = rel_bias[_rel_bucket(jnp.zeros((), jnp.int32))]
    rows_k, rows_s = [], []
    for r in range(DEC_ROWS):
        h = heads[r]
        if h >= 3 * HEADS_PER_MIXER:
            c = HEADS_PER_MIXER + h - 3 * HEADS_PER_MIXER
        elif HEADS_PER_MIXER <= h < 2 * HEADS_PER_MIXER:
            c = h - HEADS_PER_MIXER
        else:
            c = None
        rows_k.append(jnp.zeros((n_past,), F32) if c is None else per_key[:, c])
        rows_s.append(jnp.zeros((), F32) if c is None else self_b[c])
    bias_keys = jnp.stack(rows_k).reshape(DEC_ROWS, n_pages, page).transpose(1, 0, 2)
    bias_self = jnp.broadcast_to(jnp.stack(rows_s)[:, None], (DEC_ROWS, LANES))
    return bias_keys.astype(F32), bias_self.astype(F32)


def _rows8(lf, n_heads):
    x = jnp.pad(lf[:, :n_heads], ((0, 0), (0, 8 - n_heads)))
    return jnp.broadcast_to(x[:, :, None], (x.shape[0], 8, LANES))


def kernel(x_prompt, x_sample, cache_k, cache_v, cache_logf, page_table, w_in, b_f, w_out,
           ffn_w_gu, ffn_w_down, ln_g, ln_b, rel_bias, diff_lambda, diff_subln_g):
    bsz, seq, d = x_prompt.shape
    n_dec, dec_seq, _ = x_sample.shape
    assert dec_seq == 1
    depth = w_in.shape[0]
    width = N_HEADS * HEAD_DIM
    n_pool, page = cache_k.shape[1], cache_k.shape[2]
    n_past = page_table.shape[1] * page
    alpha = (2 * depth) ** 0.25

    wgu = ffn_w_gu.astype(BF16)
    wdn = ffn_w_down.astype(BF16)
    w_qkv = w_in[:, :, :3 * width].astype(BF16)
    w_f = jnp.pad(w_in[:, :, 3 * width:], ((0, 0), (0, 0), (0, GATE_COLS - HEADS_PER_MIXER))).astype(BF16)
    b_f_row = jnp.pad(b_f, ((0, 0), (0, GATE_COLS - HEADS_PER_MIXER))).reshape(depth, 1, GATE_COLS)
    w_o = w_out.astype(BF16)
    ln_g3 = ln_g.reshape(depth * 3, 1, d)
    ln_b3 = ln_b.reshape(depth * 3, 1, d)

    bias_b = _prompt_bias_tiles(rel_bias[:, :HEADS_PER_MIXER])
    bias_d = _prompt_bias_tiles(rel_bias[:, HEADS_PER_MIXER:])
    dec_consts = _decode_consts(d)
    dec_bias, dec_bias_self = _decode_bias(rel_bias, n_past, page)
    cache_k4 = jnp.transpose(cache_k, (0, 1, 3, 4, 2)).reshape(depth, n_pool, width, page)
    cache_v4 = jnp.transpose(cache_v, (0, 1, 3, 4, 2)).reshape(depth, n_pool, width, page)
    cache_lf8 = jnp.pad(jnp.swapaxes(cache_logf, 2, 3), ((0, 0), (0, 0), (0, 8 - HEADS_PER_MIXER), (0, 0)))

    tm = 512
    xp = x_prompt.reshape(bsz * seq, d)
    xs = x_sample.reshape(n_dec, d)
    ks, vs, lfs, ksd, vsd, lfsd = [], [], [], [], [], []
    for lyr in range(depth):
        lam_init = 0.8 - 0.6 * math.exp(-0.3 * lyr)
        g_head = diff_subln_g[lyr].reshape(1, HEAD_DIM)
        g_row = jnp.tile(diff_subln_g[lyr], N_HEADS).reshape(1, d)
        dlam = diff_lambda[lyr]

        xp = _ffn_ln(xp, wgu, wdn, ln_g3, ln_b3, lyr, 0, alpha, tm)
        qb, q32, kf, vf, kb, vb, lf, cum, cumt = _inproj(xp, w_qkv, w_f, b_f_row, lyr, tm, seq)
        sh = lambda a: a.reshape(bsz, seq, a.shape[-1])
        o_parts = _prompt_attention(sh(qb), sh(q32), sh(kb), sh(vb), sh(kf), sh(cum), cumt,
                                    bias_b, bias_d, g_head, dlam, lam_init)
        xp = _outproj_ln(xp, o_parts, w_o, ln_g3, ln_b3, lyr, alpha, tm)
        xp = _ffn_ln(xp, wgu, wdn, ln_g3, ln_b3, lyr, 1, alpha, tm)
        ks.append(kf.reshape(bsz, seq, N_HEADS, HEAD_DIM))
        vs.append(vf.reshape(bsz, seq, N_HEADS, HEAD_DIM))
        lfs.append(lf[:, :HEADS_PER_MIXER].reshape(bsz, seq, HEADS_PER_MIXER))

        xs = _ffn_ln(xs, wgu, wdn, ln_g3, ln_b3, lyr, 0, alpha, n_dec)
        qd, kd, vd, lfd = _inproj_dec(xs, w_qkv, w_f, b_f_row, lyr)
        od = _decode_attention(qd, kd, vd, _rows8(lfd, HEADS_PER_MIXER), cache_k4, cache_v4, cache_lf8,
                               page_table, lyr, dec_consts, dec_bias, dec_bias_self, g_row, dlam,
                               lam_init)
        xs = _outproj_ln(xs, [od], w_o, ln_g3, ln_b3, lyr, alpha, n_dec)
        xs = _ffn_ln(xs, wgu, wdn, ln_g3, ln_b3, lyr, 1, alpha, n_dec)
        ksd.append(kd.reshape(n_dec, 1, N_HEADS, HEAD_DIM))
        vsd.append(vd.reshape(n_dec, 1, N_HEADS, HEAD_DIM))
        lfsd.append(lfd[:, :HEADS_PER_MIXER].reshape(n_dec, 1, HEADS_PER_MIXER))

    return (xp.reshape(bsz, seq, d), xs.reshape(n_dec, 1, d),
            jnp.stack(ks), jnp.stack(vs), jnp.stack(lfs),
            jnp.stack(ksd), jnp.stack(vsd), jnp.stack(lfsd))
```
